```python
import math
import jax, jax.numpy as jnp
from jax import lax
import numpy as np

D_MODEL = 1024
BATCH = 2
SEQ = 16384
DEPTH = 2

GLA_HEADS = 4
GLA_DK = 256
GLA_DV = 512
GLA_HK = GLA_DK // GLA_HEADS
GLA_HV = GLA_DV // GLA_HEADS
GLA_LOWRANK = 16
GLA_TAU = 16.0
GLA_CHUNK = 64
MOBA_HEADS = 8
MOBA_HD = 64
MOBA_W = MOBA_HEADS * MOBA_HD
MOBA_BLOCK = 256
MOBA_TOPK = 3
MOBA_QCHUNK = 64
SC_W = 512
CONV_W = 3
N_BRANCH = 3
D_FF = 2816
EPS = 1e-6

IN_SPLITS = (GLA_DK, GLA_DK, GLA_DV, GLA_DV, GLA_LOWRANK,
             MOBA_W, MOBA_W, MOBA_W,
             SC_W, SC_W, SC_W,
             N_BRANCH * D_MODEL)
IN_COLS = sum(IN_SPLITS)

kernel_name = "hybrid_gla_moba_shortconv_convffn"


def rmsnorm(x, g):
    xf = x.astype(jnp.float32)
    y = xf * lax.rsqrt(jnp.mean(xf * xf, axis=-1, keepdims=True) + EPS)
    return (y * g.astype(jnp.float32)).astype(x.dtype)


def split_cols(z, sizes):
    out, start = [], 0
    for s in sizes:
        out.append(z[..., start:start + s])
        start += s
    return out


def alibi_slopes(n_heads):
    return 2.0 ** (-8.0 * (jnp.arange(n_heads, dtype=jnp.float32) + 1.0) / n_heads)


def causal_dwconv(x, w):
    c = x.shape[-1]
    return lax.conv_general_dilated(
        x, w[:, None, :].astype(x.dtype), window_strides=(1,), padding=[(CONV_W - 1, 0)],
        dimension_numbers=('NWC', 'WIO', 'NWC'), feature_group_count=c)


def gla_mixer(q, k, v, r, lr, w_lr2, b_lr, g_norm):
    bsz, t_len, _ = q.shape
    nc = t_len // GLA_CHUNK
    f32 = jnp.float32
    log_a = jax.nn.log_sigmoid((lr @ w_lr2 + b_lr).astype(f32)) / GLA_TAU

    def heads(t, hd):
        return t.astype(f32).reshape(bsz, nc, GLA_CHUNK, GLA_HEADS, hd).transpose(1, 0, 3, 2, 4)

    qc = heads(q, GLA_HK) * (GLA_HK ** -0.5)
    kc = heads(k, GLA_HK)
    vc = heads(v, GLA_HV)
    ac = heads(log_a, GLA_HK)
    causal = jnp.tril(jnp.ones((GLA_CHUNK, GLA_CHUNK), dtype=bool))

    def step(state, inp):
        qb, kb, vb, ab = inp
        bcum = jnp.cumsum(ab, axis=-2)
        diff = bcum[..., :, None, :] - bcum[..., None, :, :]
        decay = jnp.exp(jnp.where(causal[:, :, None], diff, -jnp.inf))
        att = jnp.einsum('bhid,bhjd,bhijd->bhij', qb, kb, decay)
        o = att @ vb + jnp.einsum('bhid,bhde->bhie', qb * jnp.exp(bcum), state)
        blast = bcum[..., -1:, :]
        state = (jnp.exp(blast[..., 0, :])[..., None] * state
                 + jnp.einsum('bhjd,bhje->bhde', kb * jnp.exp(blast - bcum), vb))
        return state, o

    s0 = jnp.zeros((bsz, GLA_HEADS, GLA_HK, GLA_HV), f32)
    _, o = lax.scan(step, s0, (qc, kc, vc, ac))
    o = o.transpose(1, 0, 3, 2, 4).reshape(bsz, t_len, GLA_HEADS, GLA_HV)
    o = rmsnorm(o, g_norm).reshape(bsz, t_len, GLA_DV) * jax.nn.silu(r.astype(f32))
    return o.astype(q.dtype)


def moba_mixer(q, k, v):
    bsz, t_len, _ = q.shape
    f32 = jnp.float32
    nb = -(-t_len // MOBA_BLOCK)
    tp = nb * MOBA_BLOCK
    nq = t_len // MOBA_QCHUNK
    ksel = min(MOBA_TOPK, nb)

    def heads(t):
        return t.astype(f32).reshape(bsz, t_len, MOBA_HEADS, MOBA_HD).transpose(0, 2, 1, 3)

    qh = heads(q) * (MOBA_HD ** -0.5)
    pad = ((0, 0), (0, 0), (0, tp - t_len), (0, 0))
    kp = jnp.pad(heads(k), pad)
    vp = jnp.pad(heads(v), pad)
    kblk = kp.reshape(bsz, MOBA_HEADS, nb, MOBA_BLOCK, MOBA_HD)
    vblk = vp.reshape(bsz, MOBA_HEADS, nb, MOBA_BLOCK, MOBA_HD)
    kmean = jnp.mean(kblk, axis=3)
    slopes = alibi_slopes(MOBA_HEADS)[None, :, None, None]
    bi = jnp.arange(bsz)[:, None, None, None]
    hi = jnp.arange(MOBA_HEADS)[None, :, None, None]
    offs = jnp.arange(MOBA_BLOCK)

    def one_chunk(c):
        t0 = c * MOBA_QCHUNK
        qc = lax.dynamic_slice_in_dim(qh, t0, MOBA_QCHUNK, axis=2)
        tpos = t0 + jnp.arange(MOBA_QCHUNK)
        n_own = t0 // MOBA_BLOCK
        gate = jnp.einsum('bhqd,bhnd->bhqn', qc, kmean)
        gate = jnp.where(jnp.arange(nb) < n_own, gate, -jnp.inf)
        _, sel = lax.top_k(gate, ksel)
        sel_valid = sel < n_own
        kg = kblk[bi, hi, sel]
        vg = vblk[bi, hi, sel]
        s_sel = jnp.einsum('bhqd,bhqksd->bhqks', qc, kg)
        spos = sel[..., None] * MOBA_BLOCK + offs
        s_sel = s_sel - slopes[..., None] * (tpos[:, None, None] - spos).astype(f32)
        s_sel = jnp.where(sel_valid[..., None], s_sel, -jnp.inf)
        s_sel = s_sel.reshape(bsz, MOBA_HEADS, MOBA_QCHUNK, ksel * MOBA_BLOCK)
        k_own = lax.dynamic_slice_in_dim(kp, n_own * MOBA_BLOCK, MOBA_BLOCK, axis=2)
        v_own = lax.dynamic_slice_in_dim(vp, n_own * MOBA_BLOCK, MOBA_BLOCK, axis=2)
        dist = (tpos[:, None] - (n_own * MOBA_BLOCK + offs)[None, :])
        s_own = jnp.einsum('bhqd,bhsd->bhqs', qc, k_own)
        s_own = jnp.where(dist >= 0, s_own - slopes * dist.astype(f32), -jnp.inf)
        p = jax.nn.softmax(jnp.concatenate([s_sel, s_own], axis=-1), axis=-1)
        p_sel = p[..., :ksel * MOBA_BLOCK].reshape(bsz, MOBA_HEADS, MOBA_QCHUNK, ksel, MOBA_BLOCK)
        p_own = p[..., ksel * MOBA_BLOCK:]
        return (jnp.einsum('bhqks,bhqksd->bhqd', p_sel, vg)
                + jnp.einsum('bhqs,bhsd->bhqd', p_own, v_own))

    o = lax.map(one_chunk, jnp.arange(nq))
    o = o.transpose(1, 2, 0, 3, 4).reshape(bsz, MOBA_HEADS, t_len, MOBA_HD)
    return o.transpose(0, 2, 1, 3).reshape(bsz, t_len, MOBA_W).astype(q.dtype)


def setup_inputs(seed: int = 0) -> dict:
    key = jax.random.key(seed)
    ks = jax.random.split(key, 20)
    L = DEPTH
    nrm = jax.random.normal

    def gain(k_):
        return 1.0 + 0.01 * nrm(k_, (L, D_MODEL), jnp.float32)

    return {
        "x": nrm(ks[0], (BATCH, SEQ, D_MODEL), jnp.float32),
        "g_mix_pre": gain(ks[1]),
        "w_in": nrm(ks[2], (L, D_MODEL, IN_COLS), jnp.float32) * D_MODEL ** -0.5,
        "gla_w_lr2": nrm(ks[3], (L, GLA_LOWRANK, GLA_DK), jnp.float32) * GLA_LOWRANK ** -0.5,
        "gla_b_lr": 0.1 * nrm(ks[4], (L, GLA_DK), jnp.float32),
        "gla_norm": 1.0 + 0.01 * nrm(ks[5], (L, GLA_HV), jnp.float32),
        "sc_conv_w": nrm(ks[6], (L, CONV_W, SC_W), jnp.float32) * CONV_W ** -0.5,
        "w_br_gla": nrm(ks[7], (L, GLA_DV, D_MODEL), jnp.float32) * GLA_DV ** -0.5,
        "w_br_moba": nrm(ks[8], (L, MOBA_W, D_MODEL), jnp.float32) * MOBA_W ** -0.5,
        "w_br_sc": nrm(ks[9], (L, SC_W, D_MODEL), jnp.float32) * SC_W ** -0.5,
        "w_out": nrm(ks[10], (L, D_MODEL, D_MODEL), jnp.float32) * D_MODEL ** -0.5,
        "g_mix_post": gain(ks[11]),
        "g_ffn_pre": gain(ks[12]),
        "ffn_w_up": nrm(ks[13], (L, D_MODEL, 2 * D_FF), jnp.float32) * D_MODEL ** -0.5,
        "ffn_conv_w": nrm(ks[14], (L, CONV_W, 2 * D_FF), jnp.float32) * CONV_W ** -0.5,
        "ffn_conv_b": 0.01 * nrm(ks[15], (L, 2 * D_FF), jnp.float32),
        "ffn_w_down": nrm(ks[16], (L, D_FF, D_MODEL), jnp.float32) * D_FF ** -0.5,
        "g_ffn_post": gain(ks[17]),
    }


def reference(x, g_mix_pre, w_in, gla_w_lr2, gla_b_lr, gla_norm, sc_conv_w, w_br_gla, w_br_moba,
              w_br_sc, w_out, g_mix_post, g_ffn_pre, ffn_w_up, ffn_conv_w, ffn_conv_b, ffn_w_down,
              g_ffn_post):
    bsz, t_len, _ = x.shape
    for l in range(DEPTH):
        h = rmsnorm(x, g_mix_pre[l])
        (gq, gk, gv, gr, glr, mq, mk, mv, sb, sc, sx, glog) = split_cols(h @ w_in[l], IN_SPLITS)
        br_a = gla_mixer(gq, gk, gv, gr, glr, gla_w_lr2[l], gla_b_lr[l], gla_norm[l])
        br_b = moba_mixer(mq, mk, mv)
        br_c = sb * causal_dwconv(sc * sx, sc_conv_w[l])
        gates = jax.nn.sigmoid(glog.astype(jnp.float32)).reshape(bsz, t_len, N_BRANCH, D_MODEL)
        merged = (gates[..., 0, :] * (br_a @ w_br_gla[l])
                  + gates[..., 1, :] * (br_b @ w_br_moba[l])
                  + gates[..., 2, :] * (br_c @ w_br_sc[l])).astype(x.dtype)
        x = x + rmsnorm(merged @ w_out[l], g_mix_post[l])
        h = rmsnorm(x, g_ffn_pre[l])
        u = causal_dwconv(h @ ffn_w_up[l], ffn_conv_w[l]) + ffn_conv_b[l]
        u_gate, u_up = u[..., :D_FF], u[..., D_FF:]
        y = (jax.nn.silu(u_gate) * u_up) @ ffn_w_down[l]
        x = x + rmsnorm(y, g_ffn_post[l])
    return x
```

```python
import functools

import jax
import jax.numpy as jnp
from jax import lax
from jax.experimental import pallas as pl
from jax.experimental.pallas import tpu as pltpu

F32 = jnp.float32
BF16 = jnp.bfloat16

D_MODEL = 1024
EPS = 1e-6

GLA_HEADS = 4
GLA_DK = 256
GLA_DV = 512
GLA_HK = GLA_DK // GLA_HEADS
GLA_HV = GLA_DV // GLA_HEADS
GLA_LOWRANK = 16
GLA_TAU = 16.0
GLA_CHUNK = 64
GLA_SUB = 8
GLA_LR_PAD = 128
GLA_IN = 2 * GLA_DK + 2 * GLA_DV + GLA_LR_PAD

MOBA_HEADS = 8
MOBA_HD = 64
MOBA_W = MOBA_HEADS * MOBA_HD
MOBA_BLOCK = 256
MOBA_TOPK = 3
MOBA_AUG = 2 * MOBA_HD
MOBA_MAXB = MOBA_AUG - MOBA_HD
MOBA_NEG = -65536.0

SC_W = 512
D_FF = 2816
FFN_CHUNK = 1408

VMEM_LIMIT = 56 * 1024 * 1024


def _rms(x, g):
    return x * lax.rsqrt(jnp.mean(x * x, axis=-1, keepdims=True) + EPS) * g


def _dot(a, b):
    return jnp.dot(a, b, preferred_element_type=F32)


def _dot_nt(a, b):
    return lax.dot_general(a, b, (((1,), (1,)), ((), ())), preferred_element_type=F32)


def _dot_tn(a, b):
    return lax.dot_general(a, b, (((0,), (0,)), ((), ())), preferred_element_type=F32)


def _split3(a):
    hi = a.astype(BF16)
    r1 = a - hi.astype(F32)
    mid = r1.astype(BF16)
    lo = (r1 - mid.astype(F32)).astype(BF16)
    return hi, mid, lo


def _causal_conv3(p, halo, w):
    row = lax.broadcasted_iota(jnp.int32, p.shape, 0)
    p1 = pltpu.roll(p, 1, 0)
    p1 = jnp.where(row == 0, halo[7:8, :], p1)
    p2 = pltpu.roll(p, 2, 0)
    p2 = jnp.where(row == 0, halo[6:7, :], jnp.where(row == 1, halo[7:8, :], p2))
    return w[0:1, :] * p2 + w[1:2, :] * p1 + w[2:3, :] * p


def _inproj_kernel(x_ref, g_ref, wg_ref, wm_ref, ws_ref, cw_ref,
                   gla_ref, moba_ref, brc_ref, halo_ref):
    tm = x_ref.shape[1]

    @pl.when(pl.program_id(1) == 0)
    def _():
        halo_ref[...] = jnp.zeros_like(halo_ref)

    h = _rms(x_ref[0], g_ref[...]).astype(BF16)
    gla_ref[0] = _dot(h, wg_ref[...])
    moba_ref[0] = _dot(h, wm_ref[...])
    s = _dot(h, ws_ref[...])
    sb = s[:, 0:SC_W]
    p = s[:, SC_W:2 * SC_W] * s[:, 2 * SC_W:3 * SC_W]
    y = _causal_conv3(p, halo_ref[...], cw_ref[...])
    halo_ref[...] = p[tm - 8:tm, :]
    brc_ref[0] = (sb * y).astype(BF16)


def _inproj(x, g, wg, wm, ws, cw, tm):
    bsz, t_len, _ = x.shape
    const = lambda b, t: (0, 0)
    return pl.pallas_call(
        _inproj_kernel,
        grid=(bsz, t_len // tm),
        in_specs=[
            pl.BlockSpec((1, tm, D_MODEL), lambda b, t: (b, t, 0)),
            pl.BlockSpec((1, D_MODEL), const),
            pl.BlockSpec(wg.shape, const),
            pl.BlockSpec(wm.shape, const),
            pl.BlockSpec(ws.shape, const),
            pl.BlockSpec(cw.shape, const),
        ],
        out_specs=[
            pl.BlockSpec((1, tm, GLA_IN), lambda b, t: (b, t, 0)),
            pl.BlockSpec((1, tm, 3 * MOBA_W), lambda b, t: (b, t, 0)),
            pl.BlockSpec((1, tm, SC_W), lambda b, t: (b, t, 0)),
        ],
        out_shape=[
            jax.ShapeDtypeStruct((bsz, t_len, GLA_IN), F32),
            jax.ShapeDtypeStruct((bsz, t_len, 3 * MOBA_W), F32),
            jax.ShapeDtypeStruct((bsz, t_len, SC_W), BF16),
        ],
        scratch_shapes=[pltpu.VMEM((8, SC_W), F32)],
        compiler_params=pltpu.CompilerParams(
            dimension_semantics=("arbitrary", "arbitrary"),
            vmem_limit_bytes=VMEM_LIMIT),
        name="inproj",
    )(x, g, wg, wm, ws, cw)


def _gla_kernel(z_ref, wlr_ref, blr_ref, gn_ref, o_ref, st_ref):
    tg = z_ref.shape[1]
    c = GLA_CHUNK

    @pl.when(pl.program_id(1) == 0)
    def _():
        st_ref[...] = jnp.zeros_like(st_ref)

    row = lax.broadcasted_iota(jnp.int32, (c, GLA_DK), 0)
    lane_head = lax.broadcasted_iota(jnp.int32, (c, GLA_DK), 1) // GLA_HK
    arow = lax.broadcasted_iota(jnp.int32, (c, c), 0)
    acol = lax.broadcasted_iota(jnp.int32, (c, c), 1)
    tril = (arow >= acol).astype(BF16)
    ed = lax.broadcasted_iota(jnp.int32, (GLA_DK, GLA_DV), 0) // GLA_HK
    ee = lax.broadcasted_iota(jnp.int32, (GLA_DK, GLA_DV), 1) // GLA_HV
    expand = (ed == ee).astype(BF16)
    se = lax.broadcasted_iota(jnp.int32, (GLA_DV, GLA_DK), 0) // GLA_HV
    sd = lax.broadcasted_iota(jnp.int32, (GLA_DV, GLA_DK), 1) // GLA_HK
    same_head = se == sd
    neg_inf = jnp.float32(-jnp.inf)
    w1, w2, w3 = _split3(wlr_ref[...])

    def chunk(ci, carry):
        r0 = pl.multiple_of(ci * c, c)
        rows = pl.ds(r0, c)
        q = z_ref[0, rows, 0:GLA_DK] * (GLA_HK ** -0.5)
        k = z_ref[0, rows, GLA_DK:2 * GLA_DK]
        v = z_ref[0, rows, 2 * GLA_DK:2 * GLA_DK + GLA_DV]
        r = z_ref[0, rows, 2 * GLA_DK + GLA_DV:2 * GLA_DK + 2 * GLA_DV]
        lr = z_ref[0, rows, 2 * GLA_DK + 2 * GLA_DV:GLA_IN]

        l1, l2, l3 = _split3(lr)
        zl = (_dot(l1, w1) + (_dot(l1, w2) + _dot(l2, w1))
              + (_dot(l1, w3) + _dot(l2, w2) + _dot(l3, w1))) + blr_ref[...]
        log_a = (jnp.minimum(zl, 0.0) - jnp.log1p(jnp.exp(-jnp.abs(zl)))) * (1.0 / GLA_TAU)
        a1, a2, a3 = _split3(log_a)
        bc = _dot(tril, a1) + _dot(tril, a2) + _dot(tril, a3)

        att = [jnp.zeros((c, c), F32) for _ in range(GLA_HEADS)]
        s = c // 2
        while s >= GLA_SUB:
            ngrp = c // (2 * s)
            ref = jnp.concatenate(
                [jnp.broadcast_to(bc[g * 2 * s + s:g * 2 * s + s + 1, :], (2 * s, GLA_DK))
                 for g in range(ngrp)], axis=0)
            later = (row % (2 * s)) >= s
            ql = (q * jnp.exp(jnp.where(later, bc - ref, neg_inf))).astype(BF16)
            kl = (k * jnp.exp(jnp.where(later, neg_inf, ref - bc))).astype(BF16)
            grp = (arow // (2 * s)) == (acol // (2 * s))
            for hh in range(GLA_HEADS):
                part = _dot_nt(jnp.where(lane_head == hh, ql, jnp.zeros_like(ql)), kl)
                att[hh] = att[hh] + (part if ngrp == 1 else jnp.where(grp, part, 0.0))
            s //= 2

        prods = []
        vshift = []
        for dlt in range(GLA_SUB):
            if dlt == 0:
                prods.append((q * k).astype(BF16))
                vshift.append(v)
            else:
                ok = (row % GLA_SUB) >= dlt
                kd = pltpu.roll(k, dlt, 0)
                bd = pltpu.roll(bc, dlt, 0)
                prods.append((q * kd * jnp.exp(jnp.where(ok, bc - bd, neg_inf))).astype(BF16))
                vshift.append(pltpu.roll(v, dlt, 0))
        coef = _dot(jnp.concatenate(prods, axis=0), expand)
        o = coef[0:c, :] * vshift[0]
        for dlt in range(1, GLA_SUB):
            o = o + coef[dlt * c:(dlt + 1) * c, :] * vshift[dlt]

        vb = v.astype(BF16)
        o_far = jnp.concatenate(
            [_dot(att[hh].astype(BF16), vb[:, hh * GLA_HV:(hh + 1) * GLA_HV])
             for hh in range(GLA_HEADS)], axis=1)
        st = st_ref[...]
        o_state = _dot_nt((q * jnp.exp(bc)).astype(BF16), st.astype(BF16))
        o = o + o_far + o_state

        blast = bc[c - 1:c, :]
        kdec = (k * jnp.exp(blast - bc)).astype(BF16)
        upd = _dot_tn(vb, kdec)
        st_ref[...] = st * jnp.exp(blast) + jnp.where(same_head, upd, 0.0)

        outs = []
        for hh in range(GLA_HEADS):
            oh = o[:, hh * GLA_HV:(hh + 1) * GLA_HV]
            outs.append(oh * lax.rsqrt(jnp.mean(oh * oh, axis=-1, keepdims=True) + EPS))
        on = jnp.concatenate(outs, axis=1) * gn_ref[...]
        o_ref[0, rows, :] = (on * (r * jax.nn.sigmoid(r))).astype(o_ref.dtype)
        return carry

    lax.fori_loop(0, tg // c, chunk, 0)


def _gla(z, wlr, blr, gn, tg):
    bsz, t_len, _ = z.shape
    const = lambda b, t: (0, 0)
    return pl.pallas_call(
        _gla_kernel,
        grid=(bsz, t_len // tg),
        in_specs=[
            pl.BlockSpec((1, tg, GLA_IN), lambda b, t: (b, t, 0)),
            pl.BlockSpec(wlr.shape, const),
            pl.BlockSpec(blr.shape, const),
            pl.BlockSpec(gn.shape, const),
        ],
        out_specs=pl.BlockSpec((1, tg, GLA_DV), lambda b, t: (b, t, 0)),
        out_shape=jax.ShapeDtypeStruct((bsz, t_len, GLA_DV), BF16),
        scratch_shapes=[pltpu.VMEM((GLA_DV, GLA_DK), F32)],
        compiler_params=pltpu.CompilerParams(
            dimension_semantics=("arbitrary", "arbitrary"),
            vmem_limit_bytes=VMEM_LIMIT),
        name="gla",
    )(z, wlr, blr, gn)


def _kmean_kernel(k_ref, o_ref):
    nblk = k_ref.shape[1] // MOBA_BLOCK
    k = k_ref[0].reshape(nblk, MOBA_BLOCK, MOBA_W)
    o_ref[0] = jnp.sum(k, axis=1) * (1.0 / MOBA_BLOCK)


def _kmean(moba, rows):
    bsz, t_len, _ = moba.shape
    return pl.pallas_call(
        _kmean_kernel,
        grid=(bsz, t_len // rows),
        in_specs=[pl.BlockSpec((1, rows, MOBA_W), lambda b, t: (b, t, 1))],
        out_specs=pl.BlockSpec((1, rows // MOBA_BLOCK, MOBA_W), lambda b, t: (b, t, 0)),
        out_shape=jax.ShapeDtypeStruct((bsz, t_len // MOBA_BLOCK, MOBA_W), F32),
        compiler_params=pltpu.CompilerParams(
            dimension_semantics=("arbitrary", "arbitrary"),
            vmem_limit_bytes=VMEM_LIMIT),
        name="moba_kmean",
    )(moba)


def _moba_prep_kernel(m_ref, km_ref, qa_ref, ka_ref, vt_ref):
    nb = MOBA_MAXB
    n_own = pl.program_id(1)
    n_own_f = n_own.astype(F32)
    q = m_ref[0, :, 0:MOBA_W]
    k = m_ref[0, :, MOBA_W:2 * MOBA_W]
    v = m_ref[0, :, 2 * MOBA_W:3 * MOBA_W]
    qt = q.T * (MOBA_HD ** -0.5)
    vt_ref[0, :, 0] = v.T.reshape(MOBA_HEADS, MOBA_HD, MOBA_BLOCK).astype(BF16)

    brow = lax.broadcasted_iota(jnp.int32, (nb, MOBA_BLOCK), 0).astype(F32)
    neg_inf = jnp.float32(-jnp.inf)
    lane = lax.broadcasted_iota(jnp.int32, (MOBA_BLOCK, MOBA_AUG), 1)
    pos = lax.broadcasted_iota(jnp.int32, (MOBA_BLOCK, MOBA_AUG), 0).astype(F32)

    for hh in range(MOBA_HEADS):
        slope = 2.0 ** (-8.0 * (hh + 1) / MOBA_HEADS)
        qth = qt[hh * MOBA_HD:(hh + 1) * MOBA_HD, :]
        q1, q2, q3 = _split3(qth)
        m1, m2, m3 = _split3(km_ref[0, hh])
        gate = (_dot(m1, q1) + (_dot(m1, q2) + _dot(m2, q1))
                + (_dot(m1, q3) + _dot(m2, q2) + _dot(m3, q1)))
        gate = jnp.where(brow < n_own_f, gate, neg_inf)
        chosen = jnp.zeros((nb, MOBA_BLOCK), jnp.bool_)
        for _ in range(MOBA_TOPK):
            best = jnp.max(gate, axis=0, keepdims=True)
            first = jnp.min(jnp.where(gate == best, brow, float(nb)), axis=0, keepdims=True)
            hit = brow == first
            chosen = chosen | (hit & (best > neg_inf))
            gate = jnp.where(hit, neg_inf, gate)
        term = jnp.where(chosen, (slope * MOBA_BLOCK) * (brow - n_own_f), MOBA_NEG)
        term = jnp.where(brow == n_own_f, 0.0, term)
        term = jnp.where(brow == float(nb - 1), 1.0, term)
        halves = [qth, term] if hh % 2 == 0 else [term, qth]
        qa_ref[0, hh] = jnp.concatenate(halves, axis=0).astype(BF16)

        kpair = k[:, (hh // 2) * MOBA_AUG:(hh // 2 + 1) * MOBA_AUG]
        idx = lane - MOBA_HD if hh % 2 == 0 else lane
        extra = jnp.where(idx == nb - 1, slope * pos, (idx == n_own).astype(F32))
        keep = (lane < MOBA_HD) if hh % 2 == 0 else (lane >= MOBA_HD)
        ka_ref[0, hh] = jnp.where(keep, kpair, extra).astype(BF16)


def _moba_prep(moba, km):
    bsz, t_len, _ = moba.shape
    nb = t_len // MOBA_BLOCK
    return pl.pallas_call(
        _moba_prep_kernel,
        grid=(bsz, nb),
        in_specs=[
            pl.BlockSpec((1, MOBA_BLOCK, 3 * MOBA_W), lambda b, t: (b, t, 0)),
            pl.BlockSpec((1, MOBA_HEADS, MOBA_MAXB, MOBA_HD), lambda b, t: (b, 0, 0, 0)),
        ],
        out_specs=[
            pl.BlockSpec((1, MOBA_HEADS, MOBA_AUG, MOBA_BLOCK), lambda b, t: (b, 0, 0, t)),
            pl.BlockSpec((1, MOBA_HEADS, MOBA_BLOCK, MOBA_AUG), lambda b, t: (b, 0, t, 0)),
            pl.BlockSpec((1, MOBA_HEADS, 1, MOBA_HD, MOBA_BLOCK), lambda b, t: (b, 0, t, 0, 0)),
        ],
        out_shape=[
            jax.ShapeDtypeStruct((bsz, MOBA_HEADS, MOBA_AUG, t_len), BF16),
            jax.ShapeDtypeStruct((bsz, MOBA_HEADS, t_len, MOBA_AUG), BF16),
            jax.ShapeDtypeStruct((bsz, MOBA_HEADS, nb, MOBA_HD, MOBA_BLOCK), BF16),
        ],
        compiler_params=pltpu.CompilerParams(
            dimension_semantics=("arbitrary", "arbitrary"),
            vmem_limit_bytes=VMEM_LIMIT),
        name="moba_prep",
    )(moba, km)


def _moba_attn_kernel(qa_ref, ka_ref, vt_ref, o_ref):
    qi = pl.program_id(2)
    qa = qa_ref[0, 0]

    def scores(n):
        r0 = pl.multiple_of(n * MOBA_BLOCK, MOBA_BLOCK)
        return _dot(ka_ref[0, 0, pl.ds(r0, MOBA_BLOCK), :], qa)

    st = scores(qi)
    krow = lax.broadcasted_iota(jnp.int32, st.shape, 0)
    qcol = lax.broadcasted_iota(jnp.int32, st.shape, 1)
    st = jnp.where(krow <= qcol, st, -jnp.inf)
    m0 = jnp.max(st, axis=0, keepdims=True)
    p = jnp.exp(st - m0)
    l0 = jnp.sum(p, axis=0, keepdims=True)
    acc0 = _dot(vt_ref[0, 0, qi], p.astype(BF16))

    def body(n, carry):
        m, l, acc = carry
        s = scores(n)
        m_new = jnp.maximum(m, jnp.max(s, axis=0, keepdims=True))
        alpha = jnp.exp(m - m_new)
        p = jnp.exp(s - m_new)
        l = alpha * l + jnp.sum(p, axis=0, keepdims=True)
        acc = alpha * acc + _dot(vt_ref[0, 0, n], p.astype(BF16))
        return m_new, l, acc

    _, l, acc = lax.fori_loop(0, qi, body, (m0, l0, acc0))
    o_ref[0] = (acc / l).astype(o_ref.dtype)


def _moba_attn(qa, ka, vt):
    bsz, _, _, t_len = qa.shape
    nb = t_len // MOBA_BLOCK
    return pl.pallas_call(
        _moba_attn_kernel,
        grid=(bsz, MOBA_HEADS, nb),
        in_specs=[
            pl.BlockSpec((1, 1, MOBA_AUG, MOBA_BLOCK), lambda b, h, t: (b, h, 0, t)),
            pl.BlockSpec((1, 1, t_len, MOBA_AUG), lambda b, h, t: (b, h, 0, 0)),
            pl.BlockSpec((1, 1, nb, MOBA_HD, MOBA_BLOCK), lambda b, h, t: (b, h, 0, 0, 0)),
        ],
        out_specs=pl.BlockSpec((1, MOBA_HD, MOBA_BLOCK), lambda b, h, t: (b, h, t)),
        out_shape=jax.ShapeDtypeStruct((bsz, MOBA_W, t_len), BF16),
        compiler_params=pltpu.CompilerParams(
            dimension_semantics=("arbitrary", "arbitrary", "arbitrary"),
            vmem_limit_bytes=VMEM_LIMIT),
        name="moba_attn",
    )(qa, ka, vt)


def _merge_kernel(x_ref, gpre_ref, bra_ref, brbt_ref, brc_ref, wgate_ref,
                  wa_ref, wb_ref, wc_ref, wout_ref, gpost_ref, o_ref):
    x = x_ref[0]
    h = _rms(x, gpre_ref[...]).astype(BF16)
    merged = jax.nn.sigmoid(_dot(h, wgate_ref[:, 0:D_MODEL])) * _dot(bra_ref[0], wa_ref[...])
    merged = merged + (jax.nn.sigmoid(_dot(h, wgate_ref[:, D_MODEL:2 * D_MODEL]))
                       * _dot_tn(brbt_ref[0], wb_ref[...]))
    merged = merged + (jax.nn.sigmoid(_dot(h, wgate_ref[:, 2 * D_MODEL:3 * D_MODEL]))
                       * _dot(brc_ref[0], wc_ref[...]))
    y = _dot(merged.astype(BF16), wout_ref[...])
    o_ref[0] = x + _rms(y, gpost_ref[...])


def _merge(x, gpre, bra, brbt, brc, wgate, wa, wb, wc, wout, gpost, tm):
    bsz, t_len, _ = x.shape
    const = lambda b, t: (0, 0)
    return pl.pallas_call(
        _merge_kernel,
        grid=(bsz, t_len // tm),
        in_specs=[
            pl.BlockSpec((1, tm, D_MODEL), lambda b, t: (b, t, 0)),
            pl.BlockSpec((1, D_MODEL), const),
            pl.BlockSpec((1, tm, GLA_DV), lambda b, t: (b, t, 0)),
            pl.BlockSpec((1, MOBA_W, tm), lambda b, t: (b, 0, t)),
            pl.BlockSpec((1, tm, SC_W), lambda b, t: (b, t, 0)),
            pl.BlockSpec(wgate.shape, const),
            pl.BlockSpec(wa.shape, const),
            pl.BlockSpec(wb.shape, const),
            pl.BlockSpec(wc.shape, const),
            pl.BlockSpec(wout.shape, const),
            pl.BlockSpec((1, D_MODEL), const),
        ],
        out_specs=pl.BlockSpec((1, tm, D_MODEL), lambda b, t: (b, t, 0)),
        out_shape=jax.ShapeDtypeStruct(x.shape, x.dtype),
        compiler_params=pltpu.CompilerParams(
            dimension_semantics=("arbitrary", "arbitrary"),
            vmem_limit_bytes=VMEM_LIMIT),
        name="merge",
    )(x, gpre, bra, brbt, brc, wgate, wa, wb, wc, wout, gpost)


def _ffn_kernel(x_ref, gpre_ref, wup_ref, cw_ref, cb_ref, wdn_ref, gpost_ref,
                o_ref, halo_ref):
    tm = x_ref.shape[1]

    @pl.when(pl.program_id(1) == 0)
    def _():
        halo_ref[...] = jnp.zeros_like(halo_ref)

    x = x_ref[0]
    h = _rms(x, gpre_ref[...]).astype(BF16)
    y = jnp.zeros((tm, D_MODEL), F32)
    for ci in range(D_FF // FFN_CHUNK):
        halves = []
        for base in (ci * FFN_CHUNK, D_FF + ci * FFN_CHUNK):
            cols = slice(base, base + FFN_CHUNK)
            u = _dot(h, wup_ref[:, cols])
            conv = _causal_conv3(u, halo_ref[:, cols], cw_ref[:, cols]) + cb_ref[:, cols]
            halo_ref[:, cols] = u[tm - 8:tm, :]
            halves.append(conv)
        act = (halves[0] * jax.nn.sigmoid(halves[0]) * halves[1]).astype(BF16)
        y = y + _dot(act, wdn_ref[ci * FFN_CHUNK:(ci + 1) * FFN_CHUNK, :])
    o_ref[0] = x + _rms(y, gpost_ref[...])


def _ffn(x, gpre, wup, cw, cb, wdn, gpost, tm):
    bsz, t_len, _ = x.shape
    const = lambda b, t: (0, 0)
    return pl.pallas_call(
        _ffn_kernel,
        grid=(bsz, t_len // tm),
        in_specs=[
            pl.BlockSpec((1, tm, D_MODEL), lambda b, t: (b, t, 0)),
            pl.BlockSpec((1, D_MODEL), const),
            pl.BlockSpec(wup.shape, const, pipeline_mode=pl.Buffered(1)),
            pl.BlockSpec(cw.shape, const),
            pl.BlockSpec(cb.shape, const),
            pl.BlockSpec(wdn.shape, const, pipeline_mode=pl.Buffered(1)),
            pl.BlockSpec((1, D_MODEL), const),
        ],
        out_specs=pl.BlockSpec((1, tm, D_MODEL), lambda b, t: (b, t, 0)),
        out_shape=jax.ShapeDtypeStruct(x.shape, x.dtype),
        scratch_shapes=[pltpu.VMEM((8, 2 * D_FF), F32)],
        compiler_params=pltpu.CompilerParams(
            dimension_semantics=("arbitrary", "arbitrary"),
            vmem_limit_bytes=VMEM_LIMIT),
        name="ffn",
    )(x, gpre, wup, cw, cb, wdn, gpost)


def _pad_rows(w, rows):
    return jnp.concatenate([w, jnp.zeros((rows - w.shape[0],) + w.shape[1:], w.dtype)], axis=0)


def kernel(x, g_mix_pre, w_in, gla_w_lr2, gla_b_lr, gla_norm, sc_conv_w, w_br_gla, w_br_moba,
           w_br_sc, w_out, g_mix_post, g_ffn_pre, ffn_w_up, ffn_conv_w, ffn_conv_b, ffn_w_down,
           g_ffn_post):
    bsz, t_len, _ = x.shape
    depth = w_in.shape[0]
    tm = min(512, t_len)
    tg = min(256, t_len)
    nb = t_len // MOBA_BLOCK
    gla_cols = 2 * GLA_DK + 2 * GLA_DV + GLA_LOWRANK
    moba_end = gla_cols + 3 * MOBA_W
    sc_end = moba_end + 3 * SC_W

    for l in range(depth):
        w = w_in[l]
        wg = jnp.concatenate(
            [w[:, :gla_cols], jnp.zeros((D_MODEL, GLA_LR_PAD - GLA_LOWRANK), w.dtype)],
            axis=1).astype(BF16)
        wm = w[:, gla_cols:moba_end].astype(BF16)
        ws = w[:, moba_end:sc_end].astype(BF16)
        wgate = w[:, sc_end:].astype(BF16)
        row = lambda a: a.reshape(1, -1)

        gla_in, moba, br_c = _inproj(x, row(g_mix_pre[l]), wg, wm, ws,
                                     _pad_rows(sc_conv_w[l], 8), tm)
        br_a = _gla(gla_in, _pad_rows(gla_w_lr2[l], GLA_LR_PAD), row(gla_b_lr[l]),
                    row(jnp.tile(gla_norm[l], GLA_HEADS)), tg)
        km = _kmean(moba, min(2048, t_len))
        km = km.reshape(bsz, nb, MOBA_HEADS, MOBA_HD).transpose(0, 2, 1, 3)
        km = jnp.pad(km, ((0, 0), (0, 0), (0, MOBA_MAXB - nb), (0, 0)))
        qa, ka, vt = _moba_prep(moba, km)
        br_bt = _moba_attn(qa, ka, vt)
        x = _merge(x, row(g_mix_pre[l]), br_a, br_bt, br_c, wgate,
                   w_br_gla[l].astype(BF16), w_br_moba[l].astype(BF16),
                   w_br_sc[l].astype(BF16), w_out[l].astype(BF16), row(g_mix_post[l]), tm)
        x = _ffn(x, row(g_ffn_pre[l]), ffn_w_up[l].astype(BF16), _pad_rows(ffn_conv_w[l], 8),
                 row(ffn_conv_b[l]), ffn_w_down[l].astype(BF16), row(g_ffn_post[l]), tm)
    return x
```

```python
import functools

import jax
import jax.numpy as jnp
from jax import lax
from jax.experimental import pallas as pl
from jax.experimental.pallas import tpu as pltpu

F32 = jnp.float32
BF16 = jnp.bfloat16

D_MODEL = 1024
EPS = 1e-6

GLA_HEADS = 4
GLA_DK = 256
GLA_DV = 512
GLA_HK = GLA_DK // GLA_HEADS
GLA_HV = GLA_DV // GLA_HEADS
GLA_LOWRANK = 16
GLA_TAU = 16.0
GLA_CHUNK = 64
GLA_SUB = 8
GLA_LR_PAD = 128
GLA_IN = 2 * GLA_DK + 2 * GLA_DV + GLA_LR_PAD

MOBA_HEADS = 8
MOBA_HD = 64
MOBA_W = MOBA_HEADS * MOBA_HD
MOBA_BLOCK = 256
MOBA_TOPK = 3
MOBA_AUG = 2 * MOBA_HD
MOBA_MAXB = MOBA_AUG - MOBA_HD
MOBA_NEG = -65536.0
MOBA_QT = 1024
MOBA_KT = 512
MOBA_QG = 256

SC_W = 512
D_FF = 2816
FFN_CHUNK = 1408

VMEM_LIMIT = 56 * 1024 * 1024


def _rms(x, g):
    return x * lax.rsqrt(jnp.mean(x * x, axis=-1, keepdims=True) + EPS) * g


def _dot(a, b):
    return jnp.dot(a, b, preferred_element_type=F32)


def _dot_nt(a, b):
    return lax.dot_general(a, b, (((1,), (1,)), ((), ())), preferred_element_type=F32)


def _dot_tn(a, b):
    return lax.dot_general(a, b, (((0,), (0,)), ((), ())), preferred_element_type=F32)


def _split3(a):
    hi = a.astype(BF16)
    r1 = a - hi.astype(F32)
    mid = r1.astype(BF16)
    lo = (r1 - mid.astype(F32)).astype(BF16)
    return hi, mid, lo


def _causal_conv3(p, halo, w):
    row = lax.broadcasted_iota(jnp.int32, p.shape, 0)
    p1 = pltpu.roll(p, 1, 0)
    p1 = jnp.where(row == 0, halo[7:8, :], p1)
    p2 = pltpu.roll(p, 2, 0)
    p2 = jnp.where(row == 0, halo[6:7, :], jnp.where(row == 1, halo[7:8, :], p2))
    return w[0:1, :] * p2 + w[1:2, :] * p1 + w[2:3, :] * p


def _inproj_kernel(x_ref, g_ref, wg_ref, wm_ref, ws_ref, cw_ref,
                   gla_ref, moba_ref, brc_ref, halo_ref):
    tm = x_ref.shape[1]

    @pl.when(pl.program_id(1) == 0)
    def _():
        halo_ref[...] = jnp.zeros_like(halo_ref)

    h = _rms(x_ref[0], g_ref[...]).astype(BF16)
    gla_ref[0] = _dot(h, wg_ref[...])
    moba_ref[0] = _dot(h, wm_ref[...])
    s = _dot(h, ws_ref[...])
    sb = s[:, 0:SC_W]
    p = s[:, SC_W:2 * SC_W] * s[:, 2 * SC_W:3 * SC_W]
    y = _causal_conv3(p, halo_ref[...], cw_ref[...])
    halo_ref[...] = p[tm - 8:tm, :]
    brc_ref[0] = (sb * y).astype(BF16)


def _inproj(x, g, wg, wm, ws, cw, tm):
    bsz, t_len, _ = x.shape
    const = lambda b, t: (0, 0)
    return pl.pallas_call(
        _inproj_kernel,
        grid=(bsz, t_len // tm),
        in_specs=[
            pl.BlockSpec((1, tm, D_MODEL), lambda b, t: (b, t, 0)),
            pl.BlockSpec((1, D_MODEL), const),
            pl.BlockSpec(wg.shape, const),
            pl.BlockSpec(wm.shape, const),
            pl.BlockSpec(ws.shape, const),
            pl.BlockSpec(cw.shape, const),
        ],
        out_specs=[
            pl.BlockSpec((1, tm, GLA_IN), lambda b, t: (b, t, 0)),
            pl.BlockSpec((1, tm, 3 * MOBA_W), lambda b, t: (b, t, 0)),
            pl.BlockSpec((1, tm, SC_W), lambda b, t: (b, t, 0)),
        ],
        out_shape=[
            jax.ShapeDtypeStruct((bsz, t_len, GLA_IN), F32),
            jax.ShapeDtypeStruct((bsz, t_len, 3 * MOBA_W), F32),
            jax.ShapeDtypeStruct((bsz, t_len, SC_W), BF16),
        ],
        scratch_shapes=[pltpu.VMEM((8, SC_W), F32)],
        compiler_params=pltpu.CompilerParams(
            dimension_semantics=("arbitrary", "arbitrary"),
            vmem_limit_bytes=VMEM_LIMIT),
        name="inproj",
    )(x, g, wg, wm, ws, cw)


def _gla_kernel(z_ref, wlr_ref, blr_ref, gn_ref, o_ref, st_ref):
    tg = z_ref.shape[1]
    c = GLA_CHUNK

    @pl.when(pl.program_id(1) == 0)
    def _():
        st_ref[...] = jnp.zeros_like(st_ref)

    row = lax.broadcasted_iota(jnp.int32, (c, GLA_DK), 0)
    lane_head = lax.broadcasted_iota(jnp.int32, (c, GLA_DK), 1) // GLA_HK
    arow = lax.broadcasted_iota(jnp.int32, (c, c), 0)
    acol = lax.broadcasted_iota(jnp.int32, (c, c), 1)
    tril = (arow >= acol).astype(BF16)
    ed = lax.broadcasted_iota(jnp.int32, (GLA_DK, GLA_DV), 0) // GLA_HK
    ee = lax.broadcasted_iota(jnp.int32, (GLA_DK, GLA_DV), 1) // GLA_HV
    expand = (ed == ee).astype(BF16)
    se = lax.broadcasted_iota(jnp.int32, (GLA_DV, GLA_DK), 0) // GLA_HV
    sd = lax.broadcasted_iota(jnp.int32, (GLA_DV, GLA_DK), 1) // GLA_HK
    same_head = se == sd
    neg_inf = jnp.float32(-jnp.inf)
    w1, w2, w3 = _split3(wlr_ref[...])

    def chunk(ci, carry):
        r0 = pl.multiple_of(ci * c, c)
        rows = pl.ds(r0, c)
        q = z_ref[0, rows, 0:GLA_DK] * (GLA_HK ** -0.5)
        k = z_ref[0, rows, GLA_DK:2 * GLA_DK]
        v = z_ref[0, rows, 2 * GLA_DK:2 * GLA_DK + GLA_DV]
        r = z_ref[0, rows, 2 * GLA_DK + GLA_DV:2 * GLA_DK + 2 * GLA_DV]
        lr = z_ref[0, rows, 2 * GLA_DK + 2 * GLA_DV:GLA_IN]

        l1, l2, l3 = _split3(lr)
        zl = (_dot(l1, w1) + (_dot(l1, w2) + _dot(l2, w1))
              + (_dot(l1, w3) + _dot(l2, w2) + _dot(l3, w1))) + blr_ref[...]
        log_a = (jnp.minimum(zl, 0.0) - jnp.log1p(jnp.exp(-jnp.abs(zl)))) * (1.0 / GLA_TAU)
        a1, a2, a3 = _split3(log_a)
        bc = _dot(tril, a1) + _dot(tril, a2) + _dot(tril, a3)

        att = [jnp.zeros((c, c), F32) for _ in range(GLA_HEADS)]
        s = c // 2
        while s >= GLA_SUB:
            ngrp = c // (2 * s)
            ref = jnp.concatenate(
                [jnp.broadcast_to(bc[g * 2 * s + s:g * 2 * s + s + 1, :], (2 * s, GLA_DK))
                 for g in range(ngrp)], axis=0)
            later = (row % (2 * s)) >= s
            ql = (q * jnp.exp(jnp.where(later, bc - ref, neg_inf))).astype(BF16)
            kl = (k * jnp.exp(jnp.where(later, neg_inf, ref - bc))).astype(BF16)
            grp = (arow // (2 * s)) == (acol // (2 * s))
            for hh in range(GLA_HEADS):
                part = _dot_nt(jnp.where(lane_head == hh, ql, jnp.zeros_like(ql)), kl)
                att[hh] = att[hh] + (part if ngrp == 1 else jnp.where(grp, part, 0.0))
            s //= 2

        prods = []
        vshift = []
        for dlt in range(GLA_SUB):
            if dlt == 0:
                prods.append((q * k).astype(BF16))
                vshift.append(v)
            else:
                ok = (row % GLA_SUB) >= dlt
                kd = pltpu.roll(k, dlt, 0)
                bd = pltpu.roll(bc, dlt, 0)
                prods.append((q * kd * jnp.exp(jnp.where(ok, bc - bd, neg_inf))).astype(BF16))
                vshift.append(pltpu.roll(v, dlt, 0))
        coef = _dot(jnp.concatenate(prods, axis=0), expand)
        o = coef[0:c, :] * vshift[0]
        for dlt in range(1, GLA_SUB):
            o = o + coef[dlt * c:(dlt + 1) * c, :] * vshift[dlt]

        vb = v.astype(BF16)
        o_far = jnp.concatenate(
            [_dot(att[hh].astype(BF16), vb[:, hh * GLA_HV:(hh + 1) * GLA_HV])
             for hh in range(GLA_HEADS)], axis=1)
        st = st_ref[...]
        o_state = _dot_nt((q * jnp.exp(bc)).astype(BF16), st.astype(BF16))
        o = o + o_far + o_state

        blast = bc[c - 1:c, :]
        kdec = (k * jnp.exp(blast - bc)).astype(BF16)
        upd = _dot_tn(vb, kdec)
        st_ref[...] = st * jnp.exp(blast) + jnp.where(same_head, upd, 0.0)

        outs = []
        for hh in range(GLA_HEADS):
            oh = o[:, hh * GLA_HV:(hh + 1) * GLA_HV]
            outs.append(oh * lax.rsqrt(jnp.mean(oh * oh, axis=-1, keepdims=True) + EPS))
        on = jnp.concatenate(outs, axis=1) * gn_ref[...]
        o_ref[0, rows, :] = (on * (r * jax.nn.sigmoid(r))).astype(o_ref.dtype)
        return carry

    lax.fori_loop(0, tg // c, chunk, 0)


def _gla(z, wlr, blr, gn, tg):
    bsz, t_len, _ = z.shape
    const = lambda b, t: (0, 0)
    return pl.pallas_call(
        _gla_kernel,
        grid=(bsz, t_len // tg),
        in_specs=[
            pl.BlockSpec((1, tg, GLA_IN), lambda b, t: (b, t, 0)),
            pl.BlockSpec(wlr.shape, const),
            pl.BlockSpec(blr.shape, const),
            pl.BlockSpec(gn.shape, const),
        ],
        out_specs=pl.BlockSpec((1, tg, GLA_DV), lambda b, t: (b, t, 0)),
        out_shape=jax.ShapeDtypeStruct((bsz, t_len, GLA_DV), BF16),
        scratch_shapes=[pltpu.VMEM((GLA_DV, GLA_DK), F32)],
        compiler_params=pltpu.CompilerParams(
            dimension_semantics=("arbitrary", "arbitrary"),
            vmem_limit_bytes=VMEM_LIMIT),
        name="gla",
    )(z, wlr, blr, gn)


def _kmean_kernel(k_ref, o_ref):
    nblk = k_ref.shape[1] // MOBA_BLOCK
    k = k_ref[0].reshape(nblk, MOBA_BLOCK, MOBA_W)
    o_ref[0] = jnp.sum(k, axis=1) * (1.0 / MOBA_BLOCK)


def _kmean(moba, rows):
    bsz, t_len, _ = moba.shape
    return pl.pallas_call(
        _kmean_kernel,
        grid=(bsz, t_len // rows),
        in_specs=[pl.BlockSpec((1, rows, MOBA_W), lambda b, t: (b, t, 1))],
        out_specs=pl.BlockSpec((1, rows // MOBA_BLOCK, MOBA_W), lambda b, t: (b, t, 0)),
        out_shape=jax.ShapeDtypeStruct((bsz, t_len // MOBA_BLOCK, MOBA_W), F32),
        compiler_params=pltpu.CompilerParams(
            dimension_semantics=("arbitrary", "arbitrary"),
            vmem_limit_bytes=VMEM_LIMIT),
        name="moba_kmean",
    )(moba)


def _moba_prep_kernel(m_ref, km_ref, qa_ref, ka_ref, vt_ref):
    nb = MOBA_MAXB
    n_own = pl.program_id(1)
    n_own_f = n_own.astype(F32)
    q = m_ref[0, :, 0:MOBA_W]
    k = m_ref[0, :, MOBA_W:2 * MOBA_W]
    v = m_ref[0, :, 2 * MOBA_W:3 * MOBA_W]
    qt = q.T * (MOBA_HD ** -0.5)
    vt_ref[0, :, 0] = v.T.reshape(MOBA_HEADS, MOBA_HD, MOBA_BLOCK).astype(BF16)

    brow = lax.broadcasted_iota(jnp.int32, (nb, MOBA_BLOCK), 0).astype(F32)
    neg_inf = jnp.float32(-jnp.inf)
    lane = lax.broadcasted_iota(jnp.int32, (MOBA_BLOCK, MOBA_AUG), 1)
    pos = lax.broadcasted_iota(jnp.int32, (MOBA_BLOCK, MOBA_AUG), 0).astype(F32)

    for hh in range(MOBA_HEADS):
        slope = 2.0 ** (-8.0 * (hh + 1) / MOBA_HEADS)
        qth = qt[hh * MOBA_HD:(hh + 1) * MOBA_HD, :]
        q1, q2, q3 = _split3(qth)
        m1, m2, m3 = _split3(km_ref[0, hh])
        gate = (_dot(m1, q1) + (_dot(m1, q2) + _dot(m2, q1))
                + (_dot(m1, q3) + _dot(m2, q2) + _dot(m3, q1)))
        gate = jnp.where(brow < n_own_f, gate, neg_inf)
        chosen = jnp.zeros((nb, MOBA_BLOCK), jnp.bool_)
        for _ in range(MOBA_TOPK):
            best = jnp.max(gate, axis=0, keepdims=True)
            first = jnp.min(jnp.where(gate == best, brow, float(nb)), axis=0, keepdims=True)
            hit = brow == first
            chosen = chosen | (hit & (best > neg_inf))
            gate = jnp.where(hit, neg_inf, gate)
        term = jnp.where(chosen, (slope * MOBA_BLOCK) * (brow - n_own_f), MOBA_NEG)
        term = jnp.where(brow == n_own_f, 0.0, term)
        term = jnp.where(brow == float(nb - 1), 1.0, term)
        halves = [qth, term] if hh % 2 == 0 else [term, qth]
        qa_ref[0, hh] = jnp.concatenate(halves, axis=0).astype(BF16)

        kpair = k[:, (hh // 2) * MOBA_AUG:(hh // 2 + 1) * MOBA_AUG]
        idx = lane - MOBA_HD if hh % 2 == 0 else lane
        extra = jnp.where(idx == nb - 1, slope * pos, (idx == n_own).astype(F32))
        keep = (lane < MOBA_HD) if hh % 2 == 0 else (lane >= MOBA_HD)
        ka_ref[0, hh] = jnp.where(keep, kpair, extra).astype(BF16)


def _moba_prep(moba, km):
    bsz, t_len, _ = moba.shape
    nb = t_len // MOBA_BLOCK
    return pl.pallas_call(
        _moba_prep_kernel,
        grid=(bsz, nb),
        in_specs=[
            pl.BlockSpec((1, MOBA_BLOCK, 3 * MOBA_W), lambda b, t: (b, t, 0)),
            pl.BlockSpec((1, MOBA_HEADS, MOBA_MAXB, MOBA_HD), lambda b, t: (b, 0, 0, 0)),
        ],
        out_specs=[
            pl.BlockSpec((1, MOBA_HEADS, MOBA_AUG, MOBA_BLOCK), lambda b, t: (b, 0, 0, t)),
            pl.BlockSpec((1, MOBA_HEADS, MOBA_BLOCK, MOBA_AUG), lambda b, t: (b, 0, t, 0)),
            pl.BlockSpec((1, MOBA_HEADS, 1, MOBA_HD, MOBA_BLOCK), lambda b, t: (b, 0, t, 0, 0)),
        ],
        out_shape=[
            jax.ShapeDtypeStruct((bsz, MOBA_HEADS, MOBA_AUG, t_len), BF16),
            jax.ShapeDtypeStruct((bsz, MOBA_HEADS, t_len, MOBA_AUG), BF16),
            jax.ShapeDtypeStruct((bsz, MOBA_HEADS, nb, MOBA_HD, MOBA_BLOCK), BF16),
        ],
        compiler_params=pltpu.CompilerParams(
            dimension_semantics=("arbitrary", "arbitrary"),
            vmem_limit_bytes=VMEM_LIMIT),
        name="moba_prep",
    )(moba, km)


def _moba_attn_kernel(qa_ref, ka_ref, vt_ref, o_ref):
    j = pl.program_id(2)
    qt = qa_ref.shape[3]
    ngrp = qt // MOBA_QG
    kb = MOBA_KT // MOBA_BLOCK
    dsteps = qt // MOBA_KT
    qas = [qa_ref[0, 0, :, g * MOBA_QG:(g + 1) * MOBA_QG] for g in range(ngrp)]

    def load(step):
        r0 = pl.multiple_of(step * MOBA_KT, MOBA_KT)
        k = ka_ref[0, 0, pl.ds(r0, MOBA_KT), :]
        vts = [vt_ref[0, 0, step * kb + i] for i in range(kb)]
        return k, vts

    def softmax_stage(s, m, l):
        m_new = jnp.max(s, axis=0, keepdims=True)
        alpha = None
        if m is not None:
            m_new = jnp.maximum(m, m_new)
            alpha = jnp.exp(m - m_new)
        p = jnp.exp(s - m_new)
        psum = jnp.sum(p, axis=0, keepdims=True)
        l_new = psum if alpha is None else alpha * l + psum
        return m_new, l_new, alpha, p.astype(BF16)

    def value_stage(pend, accs):
        g, pb, alpha, vts = pend
        pv = _dot(vts[0], pb[0:MOBA_BLOCK, :])
        for i in range(1, kb):
            pv = pv + _dot(vts[i], pb[i * MOBA_BLOCK:(i + 1) * MOBA_BLOCK, :])
        accs[g] = pv if alpha is None else alpha * accs[g] + pv

    base = j * dsteps
    items = []
    for d in range(dsteps):
        k_lo, k_hi = d * MOBA_KT, (d + 1) * MOBA_KT - 1
        for g in range(ngrp):
            q_lo, q_hi = g * MOBA_QG, (g + 1) * MOBA_QG - 1
            if k_lo <= q_hi:
                items.append((d, g, k_hi > q_lo))
    loaded = {d: load(base + d) for d in range(dsteps)}

    def diag_scores(item):
        d, g, masked = item
        s = _dot(loaded[d][0], qas[g])
        if masked:
            krow = lax.broadcasted_iota(jnp.int32, s.shape, 0) + d * MOBA_KT
            qcol = lax.broadcasted_iota(jnp.int32, s.shape, 1) + g * MOBA_QG
            s = jnp.where(krow <= qcol, s, -jnp.inf)
        return s

    ms, ls, accs = [None] * ngrp, [None] * ngrp, [None] * ngrp
    s_cur = diag_scores(items[0])
    k_first, _ = load(0)
    pend = None
    for i, (d, g, _) in enumerate(items):
        s_next = diag_scores(items[i + 1]) if i + 1 < len(items) else _dot(k_first, qas[0])
        ms[g], ls[g], alpha, pb = softmax_stage(s_cur, ms[g], ls[g])
        if pend is not None:
            value_stage(pend, accs)
        pend = (g, pb, alpha, loaded[d][1])
        s_cur = s_next
    assert pend[0] == ngrp - 1 and pend[2] is not None

    def body(n, carry):
        ms, ls, accs, s_cur, pend = carry
        ms, ls, accs = list(ms), list(ls), list(accs)
        pend = (ngrp - 1,) + tuple(pend)
        k, vts = load(n)
        k_after, _ = load(jnp.minimum(n + 1, base - 1))
        for g in range(ngrp):
            s_next = _dot(k, qas[g + 1]) if g + 1 < ngrp else _dot(k_after, qas[0])
            ms[g], ls[g], alpha, pb = softmax_stage(s_cur, ms[g], ls[g])
            value_stage(pend, accs)
            pend = (g, pb, alpha, vts)
            s_cur = s_next
        return tuple(ms), tuple(ls), tuple(accs), s_cur, pend[1:]

    carry = (tuple(ms), tuple(ls), tuple(accs), s_cur, pend[1:])
    ms, ls, accs, _, pend = lax.fori_loop(0, base, body, carry)
    accs = list(accs)
    value_stage((ngrp - 1,) + tuple(pend), accs)
    for g in range(ngrp):
        o_ref[0, :, g * MOBA_QG:(g + 1) * MOBA_QG] = (accs[g] / ls[g]).astype(o_ref.dtype)


def _moba_attn(qa, ka, vt):
    bsz, _, _, t_len = qa.shape
    nb = t_len // MOBA_BLOCK
    qt = min(MOBA_QT, t_len)
    return pl.pallas_call(
        _moba_attn_kernel,
        grid=(bsz, MOBA_HEADS, t_len // qt),
        in_specs=[
            pl.BlockSpec((1, 1, MOBA_AUG, qt), lambda b, h, t: (b, h, 0, t)),
            pl.BlockSpec((1, 1, t_len, MOBA_AUG), lambda b, h, t: (b, h, 0, 0)),
            pl.BlockSpec((1, 1, nb, MOBA_HD, MOBA_BLOCK), lambda b, h, t: (b, h, 0, 0, 0)),
        ],
        out_specs=pl.BlockSpec((1, MOBA_HD, qt), lambda b, h, t: (b, h, t)),
        out_shape=jax.ShapeDtypeStruct((bsz, MOBA_W, t_len), BF16),
        compiler_params=pltpu.CompilerParams(
            dimension_semantics=("arbitrary", "arbitrary", "arbitrary"),
            vmem_limit_bytes=VMEM_LIMIT),
        name="moba_attn",
    )(qa, ka, vt)


def _merge_kernel(x_ref, gpre_ref, bra_ref, brbt_ref, brc_ref, wgate_ref,
                  wa_ref, wb_ref, wc_ref, wout_ref, gpost_ref, o_ref):
    x = x_ref[0]
    h = _rms(x, gpre_ref[...]).astype(BF16)
    merged = jax.nn.sigmoid(_dot(h, wgate_ref[:, 0:D_MODEL])) * _dot(bra_ref[0], wa_ref[...])
    merged = merged + (jax.nn.sigmoid(_dot(h, wgate_ref[:, D_MODEL:2 * D_MODEL]))
                       * _dot_tn(brbt_ref[0], wb_ref[...]))
    merged = merged + (jax.nn.sigmoid(_dot(h, wgate_ref[:, 2 * D_MODEL:3 * D_MODEL]))
                       * _dot(brc_ref[0], wc_ref[...]))
    y = _dot(merged.astype(BF16), wout_ref[...])
    o_ref[0] = x + _rms(y, gpost_ref[...])


def _merge(x, gpre, bra, brbt, brc, wgate, wa, wb, wc, wout, gpost, tm):
    bsz, t_len, _ = x.shape
    const = lambda b, t: (0, 0)
    return pl.pallas_call(
        _merge_kernel,
        grid=(bsz, t_len // tm),
        in_specs=[
            pl.BlockSpec((1, tm, D_MODEL), lambda b, t: (b, t, 0)),
            pl.BlockSpec((1, D_MODEL), const),
            pl.BlockSpec((1, tm, GLA_DV), lambda b, t: (b, t, 0)),
            pl.BlockSpec((1, MOBA_W, tm), lambda b, t: (b, 0, t)),
            pl.BlockSpec((1, tm, SC_W), lambda b, t: (b, t, 0)),
            pl.BlockSpec(wgate.shape, const),
            pl.BlockSpec(wa.shape, const),
            pl.BlockSpec(wb.shape, const),
            pl.BlockSpec(wc.shape, const),
            pl.BlockSpec(wout.shape, const),
            pl.BlockSpec((1, D_MODEL), const),
        ],
        out_specs=pl.BlockSpec((1, tm, D_MODEL), lambda b, t: (b, t, 0)),
        out_shape=jax.ShapeDtypeStruct(x.shape, x.dtype),
        compiler_params=pltpu.CompilerParams(
            dimension_semantics=("arbitrary", "arbitrary"),
            vmem_limit_bytes=VMEM_LIMIT),
        name="merge",
    )(x, gpre, bra, brbt, brc, wgate, wa, wb, wc, wout, gpost)


def _ffn_kernel(x_ref, gpre_ref, wup_ref, cw_ref, cb_ref, wdn_ref, gpost_ref,
                o_ref, halo_ref):
    tm = x_ref.shape[1]

    @pl.when(pl.program_id(1) == 0)
    def _():
        halo_ref[...] = jnp.zeros_like(halo_ref)

    x = x_ref[0]
    h = _rms(x, gpre_ref[...]).astype(BF16)
    y = jnp.zeros((tm, D_MODEL), F32)
    for ci in range(D_FF // FFN_CHUNK):
        halves = []
        for base in (ci * FFN_CHUNK, D_FF + ci * FFN_CHUNK):
            cols = slice(base, base + FFN_CHUNK)
            u = _dot(h, wup_ref[:, cols])
            conv = _causal_conv3(u, halo_ref[:, cols], cw_ref[:, cols]) + cb_ref[:, cols]
            halo_ref[:, cols] = u[tm - 8:tm, :]
            halves.append(conv)
        act = (halves[0] * jax.nn.sigmoid(halves[0]) * halves[1]).astype(BF16)
        y = y + _dot(act, wdn_ref[ci * FFN_CHUNK:(ci + 1) * FFN_CHUNK, :])
    o_ref[0] = x + _rms(y, gpost_ref[...])


def _ffn(x, gpre, wup, cw, cb, wdn, gpost, tm):
    bsz, t_len, _ = x.shape
    const = lambda b, t: (0, 0)
    return pl.pallas_call(
        _ffn_kernel,
        grid=(bsz, t_len // tm),
        in_specs=[
            pl.BlockSpec((1, tm, D_MODEL), lambda b, t: (b, t, 0)),
            pl.BlockSpec((1, D_MODEL), const),
            pl.BlockSpec(wup.shape, const, pipeline_mode=pl.Buffered(1)),
            pl.BlockSpec(cw.shape, const),
            pl.BlockSpec(cb.shape, const),
            pl.BlockSpec(wdn.shape, const, pipeline_mode=pl.Buffered(1)),
            pl.BlockSpec((1, D_MODEL), const),
        ],
        out_specs=pl.BlockSpec((1, tm, D_MODEL), lambda b, t: (b, t, 0)),
        out_shape=jax.ShapeDtypeStruct(x.shape, x.dtype),
        scratch_shapes=[pltpu.VMEM((8, 2 * D_FF), F32)],
        compiler_params=pltpu.CompilerParams(
            dimension_semantics=("arbitrary", "arbitrary"),
            vmem_limit_bytes=VMEM_LIMIT),
        name="ffn",
    )(x, gpre, wup, cw, cb, wdn, gpost)


def _pad_rows(w, rows):
    return jnp.concatenate([w, jnp.zeros((rows - w.shape[0],) + w.shape[1:], w.dtype)], axis=0)


def kernel(x, g_mix_pre, w_in, gla_w_lr2, gla_b_lr, gla_norm, sc_conv_w, w_br_gla, w_br_moba,
           w_br_sc, w_out, g_mix_post, g_ffn_pre, ffn_w_up, ffn_conv_w, ffn_conv_b, ffn_w_down,
           g_ffn_post):
    bsz, t_len, _ = x.shape
    depth = w_in.shape[0]
    tm = min(512, t_len)
    tg = min(256, t_len)
    nb = t_len // MOBA_BLOCK
    gla_cols = 2 * GLA_DK + 2 * GLA_DV + GLA_LOWRANK
    moba_end = gla_cols + 3 * MOBA_W
    sc_end = moba_end + 3 * SC_W

    for l in range(depth):
        w = w_in[l]
        wg = jnp.concatenate(
            [w[:, :gla_cols], jnp.zeros((D_MODEL, GLA_LR_PAD - GLA_LOWRANK), w.dtype)],
            axis=1).astype(BF16)
        wm = w[:, gla_cols:moba_end].astype(BF16)
        ws = w[:, moba_end:sc_end].astype(BF16)
        wgate = w[:, sc_end:].astype(BF16)
        row = lambda a: a.reshape(1, -1)

        gla_in, moba, br_c = _inproj(x, row(g_mix_pre[l]), wg, wm, ws,
                                     _pad_rows(sc_conv_w[l], 8), tm)
        br_a = _gla(gla_in, _pad_rows(gla_w_lr2[l], GLA_LR_PAD), row(gla_b_lr[l]),
                    row(jnp.tile(gla_norm[l], GLA_HEADS)), tg)
        km = _kmean(moba, min(2048, t_len))
        km = km.reshape(bsz, nb, MOBA_HEADS, MOBA_HD).transpose(0, 2, 1, 3)
        km = jnp.pad(km, ((0, 0), (0, 0), (0, MOBA_MAXB - nb), (0, 0)))
        qa, ka, vt = _moba_prep(moba, km)
        br_bt = _moba_attn(qa, ka, vt)
        x = _merge(x, row(g_mix_pre[l]), br_a, br_bt, br_c, wgate,
                   w_br_gla[l].astype(BF16), w_br_moba[l].astype(BF16),
                   w_br_sc[l].astype(BF16), w_out[l].astype(BF16), row(g_mix_post[l]), tm)
        x = _ffn(x, row(g_ffn_pre[l]), ffn_w_up[l].astype(BF16), _pad_rows(ffn_conv_w[l], 8),
                 row(ffn_conv_b[l]), ffn_w_down[l].astype(BF16), row(g_ffn_post[l]), tm)
    return x
```

```python
import functools

import jax
import jax.numpy as jnp
from jax import lax
from jax.experimental import pallas as pl
from jax.experimental.pallas import tpu as pltpu

F32 = jnp.float32
BF16 = jnp.bfloat16

D_MODEL = 1024
EPS = 1e-6

GLA_HEADS = 4
GLA_DK = 256
GLA_DV = 512
GLA_HK = GLA_DK // GLA_HEADS
GLA_HV = GLA_DV // GLA_HEADS
GLA_LOWRANK = 16
GLA_TAU = 16.0
GLA_CHUNK = 64
GLA_SUB = 8
GLA_LR_PAD = 128
GLA_IN = 2 * GLA_DK + 2 * GLA_DV + GLA_LR_PAD

MOBA_HEADS = 8
MOBA_HD = 64
MOBA_W = MOBA_HEADS * MOBA_HD
MOBA_BLOCK = 256
MOBA_TOPK = 3
MOBA_AUG = 2 * MOBA_HD
MOBA_MAXB = MOBA_AUG - MOBA_HD
MOBA_NEG = -65536.0
MOBA_QT = 2048
MOBA_KT = 512
MOBA_QG = 256
MOBA_VROWS = MOBA_HD + 16

SC_W = 512
D_FF = 2816
FFN_CHUNK = 1408

VMEM_LIMIT = 56 * 1024 * 1024


def _rms(x, g):
    return x * lax.rsqrt(jnp.mean(x * x, axis=-1, keepdims=True) + EPS) * g


def _dot(a, b):
    return jnp.dot(a, b, preferred_element_type=F32)


def _dot_nt(a, b):
    return lax.dot_general(a, b, (((1,), (1,)), ((), ())), preferred_element_type=F32)


def _dot_tn(a, b):
    return lax.dot_general(a, b, (((0,), (0,)), ((), ())), preferred_element_type=F32)


def _split3(a):
    hi = a.astype(BF16)
    r1 = a - hi.astype(F32)
    mid = r1.astype(BF16)
    lo = (r1 - mid.astype(F32)).astype(BF16)
    return hi, mid, lo


def _causal_conv3(p, halo, w):
    row = lax.broadcasted_iota(jnp.int32, p.shape, 0)
    p1 = pltpu.roll(p, 1, 0)
    p1 = jnp.where(row == 0, halo[7:8, :], p1)
    p2 = pltpu.roll(p, 2, 0)
    p2 = jnp.where(row == 0, halo[6:7, :], jnp.where(row == 1, halo[7:8, :], p2))
    return w[0:1, :] * p2 + w[1:2, :] * p1 + w[2:3, :] * p


def _inproj_kernel(x_ref, g_ref, wg_ref, wm_ref, ws_ref, cw_ref,
                   gla_ref, moba_ref, brc_ref, halo_ref):
    tm = x_ref.shape[1]

    @pl.when(pl.program_id(1) == 0)
    def _():
        halo_ref[...] = jnp.zeros_like(halo_ref)

    h = _rms(x_ref[0], g_ref[...]).astype(BF16)
    gla_ref[0] = _dot(h, wg_ref[...])
    moba_ref[0] = _dot(h, wm_ref[...])
    s = _dot(h, ws_ref[...])
    sb = s[:, 0:SC_W]
    p = s[:, SC_W:2 * SC_W] * s[:, 2 * SC_W:3 * SC_W]
    y = _causal_conv3(p, halo_ref[...], cw_ref[...])
    halo_ref[...] = p[tm - 8:tm, :]
    brc_ref[0] = (sb * y).astype(BF16)


def _inproj(x, g, wg, wm, ws, cw, tm):
    bsz, t_len, _ = x.shape
    const = lambda b, t: (0, 0)
    return pl.pallas_call(
        _inproj_kernel,
        grid=(bsz, t_len // tm),
        in_specs=[
            pl.BlockSpec((1, tm, D_MODEL), lambda b, t: (b, t, 0)),
            pl.BlockSpec((1, D_MODEL), const),
            pl.BlockSpec(wg.shape, const),
            pl.BlockSpec(wm.shape, const),
            pl.BlockSpec(ws.shape, const),
            pl.BlockSpec(cw.shape, const),
        ],
        out_specs=[
            pl.BlockSpec((1, tm, GLA_IN), lambda b, t: (b, t, 0)),
            pl.BlockSpec((1, tm, 3 * MOBA_W), lambda b, t: (b, t, 0)),
            pl.BlockSpec((1, tm, SC_W), lambda b, t: (b, t, 0)),
        ],
        out_shape=[
            jax.ShapeDtypeStruct((bsz, t_len, GLA_IN), F32),
            jax.ShapeDtypeStruct((bsz, t_len, 3 * MOBA_W), F32),
            jax.ShapeDtypeStruct((bsz, t_len, SC_W), BF16),
        ],
        scratch_shapes=[pltpu.VMEM((8, SC_W), F32)],
        compiler_params=pltpu.CompilerParams(
            dimension_semantics=("arbitrary", "arbitrary"),
            vmem_limit_bytes=VMEM_LIMIT),
        name="inproj",
    )(x, g, wg, wm, ws, cw)


def _gla_kernel(z_ref, wlr_ref, blr_ref, gn_ref, o_ref, st_ref):
    tg = z_ref.shape[1]
    c = GLA_CHUNK

    @pl.when(pl.program_id(1) == 0)
    def _():
        st_ref[...] = jnp.zeros_like(st_ref)

    row = lax.broadcasted_iota(jnp.int32, (c, GLA_DK), 0)
    lane_head = lax.broadcasted_iota(jnp.int32, (c, GLA_DK), 1) // GLA_HK
    arow = lax.broadcasted_iota(jnp.int32, (c, c), 0)
    acol = lax.broadcasted_iota(jnp.int32, (c, c), 1)
    tril = (arow >= acol).astype(BF16)
    ed = lax.broadcasted_iota(jnp.int32, (GLA_DK, GLA_DV), 0) // GLA_HK
    ee = lax.broadcasted_iota(jnp.int32, (GLA_DK, GLA_DV), 1) // GLA_HV
    expand = (ed == ee).astype(BF16)
    se = lax.broadcasted_iota(jnp.int32, (GLA_DV, GLA_DK), 0) // GLA_HV
    sd = lax.broadcasted_iota(jnp.int32, (GLA_DV, GLA_DK), 1) // GLA_HK
    same_head = se == sd
    neg_inf = jnp.float32(-jnp.inf)
    w1, w2, w3 = _split3(wlr_ref[...])

    def chunk(ci, carry):
        r0 = pl.multiple_of(ci * c, c)
        rows = pl.ds(r0, c)
        q = z_ref[0, rows, 0:GLA_DK] * (GLA_HK ** -0.5)
        k = z_ref[0, rows, GLA_DK:2 * GLA_DK]
        v = z_ref[0, rows, 2 * GLA_DK:2 * GLA_DK + GLA_DV]
        r = z_ref[0, rows, 2 * GLA_DK + GLA_DV:2 * GLA_DK + 2 * GLA_DV]
        lr = z_ref[0, rows, 2 * GLA_DK + 2 * GLA_DV:GLA_IN]

        l1, l2, l3 = _split3(lr)
        zl = (_dot(l1, w1) + (_dot(l1, w2) + _dot(l2, w1))
              + (_dot(l1, w3) + _dot(l2, w2) + _dot(l3, w1))) + blr_ref[...]
        log_a = (jnp.minimum(zl, 0.0) - jnp.log1p(jnp.exp(-jnp.abs(zl)))) * (1.0 / GLA_TAU)
        a1, a2, a3 = _split3(log_a)
        bc = _dot(tril, a1) + _dot(tril, a2) + _dot(tril, a3)

        att = [jnp.zeros((c, c), F32) for _ in range(GLA_HEADS)]
        s = c // 2
        while s >= GLA_SUB:
            ngrp = c // (2 * s)
            ref = jnp.concatenate(
                [jnp.broadcast_to(bc[g * 2 * s + s:g * 2 * s + s + 1, :], (2 * s, GLA_DK))
                 for g in range(ngrp)], axis=0)
            later = (row % (2 * s)) >= s
            ql = (q * jnp.exp(jnp.where(later, bc - ref, neg_inf))).astype(BF16)
            kl = (k * jnp.exp(jnp.where(later, neg_inf, ref - bc))).astype(BF16)
            grp = (arow // (2 * s)) == (acol // (2 * s))
            for hh in range(GLA_HEADS):
                part = _dot_nt(jnp.where(lane_head == hh, ql, jnp.zeros_like(ql)), kl)
                att[hh] = att[hh] + (part if ngrp == 1 else jnp.where(grp, part, 0.0))
            s //= 2

        prods = []
        vshift = []
        for dlt in range(GLA_SUB):
            if dlt == 0:
                prods.append((q * k).astype(BF16))
                vshift.append(v)
            else:
                ok = (row % GLA_SUB) >= dlt
                kd = pltpu.roll(k, dlt, 0)
                bd = pltpu.roll(bc, dlt, 0)
                prods.append((q * kd * jnp.exp(jnp.where(ok, bc - bd, neg_inf))).astype(BF16))
                vshift.append(pltpu.roll(v, dlt, 0))
        coef = _dot(jnp.concatenate(prods, axis=0), expand)
        o = coef[0:c, :] * vshift[0]
        for dlt in range(1, GLA_SUB):
            o = o + coef[dlt * c:(dlt + 1) * c, :] * vshift[dlt]

        vb = v.astype(BF16)
        o_far = jnp.concatenate(
            [_dot(att[hh].astype(BF16), vb[:, hh * GLA_HV:(hh + 1) * GLA_HV])
             for hh in range(GLA_HEADS)], axis=1)
        st = st_ref[...]
        o_state = _dot_nt((q * jnp.exp(bc)).astype(BF16), st.astype(BF16))
        o = o + o_far + o_state

        blast = bc[c - 1:c, :]
        kdec = (k * jnp.exp(blast - bc)).astype(BF16)
        upd = _dot_tn(vb, kdec)
        st_ref[...] = st * jnp.exp(blast) + jnp.where(same_head, upd, 0.0)

        outs = []
        for hh in range(GLA_HEADS):
            oh = o[:, hh * GLA_HV:(hh + 1) * GLA_HV]
            outs.append(oh * lax.rsqrt(jnp.mean(oh * oh, axis=-1, keepdims=True) + EPS))
        on = jnp.concatenate(outs, axis=1) * gn_ref[...]
        o_ref[0, rows, :] = (on * (r * jax.nn.sigmoid(r))).astype(o_ref.dtype)
        return carry

    lax.fori_loop(0, tg // c, chunk, 0)


def _gla(z, wlr, blr, gn, tg):
    bsz, t_len, _ = z.shape
    const = lambda b, t: (0, 0)
    return pl.pallas_call(
        _gla_kernel,
        grid=(bsz, t_len // tg),
        in_specs=[
            pl.BlockSpec((1, tg, GLA_IN), lambda b, t: (b, t, 0)),
            pl.BlockSpec(wlr.shape, const),
            pl.BlockSpec(blr.shape, const),
            pl.BlockSpec(gn.shape, const),
        ],
        out_specs=pl.BlockSpec((1, tg, GLA_DV), lambda b, t: (b, t, 0)),
        out_shape=jax.ShapeDtypeStruct((bsz, t_len, GLA_DV), BF16),
        scratch_shapes=[pltpu.VMEM((GLA_DV, GLA_DK), F32)],
        compiler_params=pltpu.CompilerParams(
            dimension_semantics=("arbitrary", "arbitrary"),
            vmem_limit_bytes=VMEM_LIMIT),
        name="gla",
    )(z, wlr, blr, gn)


def _kmean_kernel(k_ref, o_ref):
    nblk = k_ref.shape[1] // MOBA_BLOCK
    k = k_ref[0].reshape(nblk, MOBA_BLOCK, MOBA_W)
    o_ref[0] = jnp.sum(k, axis=1) * (1.0 / MOBA_BLOCK)


def _kmean(moba, rows):
    bsz, t_len, _ = moba.shape
    return pl.pallas_call(
        _kmean_kernel,
        grid=(bsz, t_len // rows),
        in_specs=[pl.BlockSpec((1, rows, MOBA_W), lambda b, t: (b, t, 1))],
        out_specs=pl.BlockSpec((1, rows // MOBA_BLOCK, MOBA_W), lambda b, t: (b, t, 0)),
        out_shape=jax.ShapeDtypeStruct((bsz, t_len // MOBA_BLOCK, MOBA_W), F32),
        compiler_params=pltpu.CompilerParams(
            dimension_semantics=("arbitrary", "arbitrary"),
            vmem_limit_bytes=VMEM_LIMIT),
        name="moba_kmean",
    )(moba)


def _moba_prep_kernel(m_ref, km_ref, qa_ref, ka_ref, vt_ref):
    nb = MOBA_MAXB
    n_own = pl.program_id(1)
    n_own_f = n_own.astype(F32)
    q = m_ref[0, :, 0:MOBA_W]
    k = m_ref[0, :, MOBA_W:2 * MOBA_W]
    v = m_ref[0, :, 2 * MOBA_W:3 * MOBA_W]
    qt = q.T * (MOBA_HD ** -0.5)
    vt_ref[0, :, 0, 0:MOBA_HD, :] = v.T.reshape(MOBA_HEADS, MOBA_HD, MOBA_BLOCK).astype(BF16)
    tail = lax.broadcasted_iota(jnp.int32, (MOBA_HEADS, MOBA_VROWS - MOBA_HD, MOBA_BLOCK), 1)
    vt_ref[0, :, 0, MOBA_HD:MOBA_VROWS, :] = (tail == 0).astype(BF16)

    brow = lax.broadcasted_iota(jnp.int32, (nb, MOBA_BLOCK), 0).astype(F32)
    neg_inf = jnp.float32(-jnp.inf)
    lane = lax.broadcasted_iota(jnp.int32, (MOBA_BLOCK, MOBA_AUG), 1)
    pos = lax.broadcasted_iota(jnp.int32, (MOBA_BLOCK, MOBA_AUG), 0).astype(F32)

    for hh in range(MOBA_HEADS):
        slope = 2.0 ** (-8.0 * (hh + 1) / MOBA_HEADS)
        qth = qt[hh * MOBA_HD:(hh + 1) * MOBA_HD, :]
        q1, q2, q3 = _split3(qth)
        m1, m2, m3 = _split3(km_ref[0, hh])
        gate = (_dot(m1, q1) + (_dot(m1, q2) + _dot(m2, q1))
                + (_dot(m1, q3) + _dot(m2, q2) + _dot(m3, q1)))
        gate = jnp.where(brow < n_own_f, gate, neg_inf)
        chosen = jnp.zeros((nb, MOBA_BLOCK), jnp.bool_)
        for _ in range(MOBA_TOPK):
            best = jnp.max(gate, axis=0, keepdims=True)
            first = jnp.min(jnp.where(gate == best, brow, float(nb)), axis=0, keepdims=True)
            hit = brow == first
            chosen = chosen | (hit & (best > neg_inf))
            gate = jnp.where(hit, neg_inf, gate)
        term = jnp.where(chosen, (slope * MOBA_BLOCK) * (brow - n_own_f), MOBA_NEG)
        term = jnp.where(brow == n_own_f, 0.0, term)
        term = jnp.where(brow == float(nb - 1), 1.0, term)
        halves = [qth, term] if hh % 2 == 0 else [term, qth]
        qa_ref[0, hh] = jnp.concatenate(halves, axis=0).astype(BF16)

        kpair = k[:, (hh // 2) * MOBA_AUG:(hh // 2 + 1) * MOBA_AUG]
        idx = lane - MOBA_HD if hh % 2 == 0 else lane
        extra = jnp.where(idx == nb - 1, slope * pos, (idx == n_own).astype(F32))
        keep = (lane < MOBA_HD) if hh % 2 == 0 else (lane >= MOBA_HD)
        ka_ref[0, hh] = jnp.where(keep, kpair, extra).astype(BF16)


def _moba_prep(moba, km):
    bsz, t_len, _ = moba.shape
    nb = t_len // MOBA_BLOCK
    return pl.pallas_call(
        _moba_prep_kernel,
        grid=(bsz, nb),
        in_specs=[
            pl.BlockSpec((1, MOBA_BLOCK, 3 * MOBA_W), lambda b, t: (b, t, 0)),
            pl.BlockSpec((1, MOBA_HEADS, MOBA_MAXB, MOBA_HD), lambda b, t: (b, 0, 0, 0)),
        ],
        out_specs=[
            pl.BlockSpec((1, MOBA_HEADS, MOBA_AUG, MOBA_BLOCK), lambda b, t: (b, 0, 0, t)),
            pl.BlockSpec((1, MOBA_HEADS, MOBA_BLOCK, MOBA_AUG), lambda b, t: (b, 0, t, 0)),
            pl.BlockSpec((1, MOBA_HEADS, 1, MOBA_VROWS, MOBA_BLOCK), lambda b, t: (b, 0, t, 0, 0)),
        ],
        out_shape=[
            jax.ShapeDtypeStruct((bsz, MOBA_HEADS, MOBA_AUG, t_len), BF16),
            jax.ShapeDtypeStruct((bsz, MOBA_HEADS, t_len, MOBA_AUG), BF16),
            jax.ShapeDtypeStruct((bsz, MOBA_HEADS, nb, MOBA_VROWS, MOBA_BLOCK), BF16),
        ],
        compiler_params=pltpu.CompilerParams(
            dimension_semantics=("arbitrary", "arbitrary"),
            vmem_limit_bytes=VMEM_LIMIT),
        name="moba_prep",
    )(moba, km)


def _moba_attn_kernel(qa_ref, ka_ref, vt_ref, o_ref, s_scr, p_scr, acc_scr):
    j = pl.program_id(2)
    qt = qa_ref.shape[3]
    ngrp = qt // MOBA_QG
    kb = MOBA_KT // MOBA_BLOCK
    dsteps = qt // MOBA_KT
    base = j * dsteps

    def keys(step):
        r0 = pl.multiple_of(step * MOBA_KT, MOBA_KT)
        return ka_ref[0, 0, pl.ds(r0, MOBA_KT), :]

    def score_stage(buf, step, g, mask_from=None):
        s = _dot(keys(step), qa_ref[0, 0, :, g * MOBA_QG:(g + 1) * MOBA_QG])
        if mask_from is not None:
            krow = lax.broadcasted_iota(jnp.int32, s.shape, 0) + mask_from
            qcol = lax.broadcasted_iota(jnp.int32, s.shape, 1) + g * MOBA_QG
            s = jnp.where(krow <= qcol, s, -jnp.inf)
        s_scr[buf] = s

    def softmax_stage(buf, m):
        s = s_scr[buf]
        m_new = jnp.max(s, axis=0, keepdims=True)
        alpha = None
        if m is not None:
            m_new = jnp.maximum(m, m_new)
            alpha = jnp.exp(m - m_new)
        p_scr[buf] = jnp.exp(s - m_new).astype(BF16)
        return m_new, alpha

    def value_stage(buf, g, alpha, step):
        pv = _dot(vt_ref[0, 0, step * kb], p_scr[buf, 0:MOBA_BLOCK, :])
        for i in range(1, kb):
            pv = pv + _dot(vt_ref[0, 0, step * kb + i],
                           p_scr[buf, i * MOBA_BLOCK:(i + 1) * MOBA_BLOCK, :])
        acc_scr[g] = pv if alpha is None else alpha * acc_scr[g] + pv

    items = []
    for d in range(dsteps):
        k_lo, k_hi = d * MOBA_KT, (d + 1) * MOBA_KT - 1
        for g in range(ngrp):
            q_lo, q_hi = g * MOBA_QG, (g + 1) * MOBA_QG - 1
            if k_lo <= q_hi:
                items.append((d, g, k_lo if k_hi > q_lo else None))
    assert ngrp % 2 == 0 and len(items) % 2 == 0 and items[-1][1] == ngrp - 1

    ms = [None] * ngrp
    score_stage(0, base + items[0][0], items[0][1], items[0][2])
    pend = None
    for i, (d, g, _) in enumerate(items):
        if i + 1 < len(items):
            score_stage((i + 1) % 2, base + items[i + 1][0], items[i + 1][1], items[i + 1][2])
        else:
            score_stage((i + 1) % 2, 0, 0)
        ms[g], alpha = softmax_stage(i % 2, ms[g])
        if pend is not None:
            value_stage((i - 1) % 2, *pend)
        pend = (g, alpha, base + d)

    def body(n, carry):
        ms, alpha_p, step_p = carry
        ms = list(ms)
        pend = (ngrp - 1, alpha_p, step_p)
        for g in range(ngrp):
            if g + 1 < ngrp:
                score_stage((g + 1) % 2, n, g + 1)
            else:
                score_stage(0, jnp.minimum(n + 1, base - 1), 0)
            ms[g], alpha = softmax_stage(g % 2, ms[g])
            value_stage((g - 1) % 2, *pend)
            pend = (g, alpha, n)
        return tuple(ms), pend[1], pend[2]

    _, alpha_p, step_p = lax.fori_loop(0, base, body, (tuple(ms), pend[1], pend[2]))
    value_stage(1, ngrp - 1, alpha_p, step_p)
    for g in range(ngrp):
        num, den = acc_scr[g, 0:MOBA_HD, :], acc_scr[g, MOBA_HD:MOBA_HD + 1, :]
        o_ref[0, :, g * MOBA_QG:(g + 1) * MOBA_QG] = (num / den).astype(o_ref.dtype)


def _moba_attn(qa, ka, vt):
    bsz, _, _, t_len = qa.shape
    nb = t_len // MOBA_BLOCK
    qt = min(MOBA_QT, t_len)
    return pl.pallas_call(
        _moba_attn_kernel,
        grid=(bsz, MOBA_HEADS, t_len // qt),
        in_specs=[
            pl.BlockSpec((1, 1, MOBA_AUG, qt), lambda b, h, t: (b, h, 0, t)),
            pl.BlockSpec((1, 1, t_len, MOBA_AUG), lambda b, h, t: (b, h, 0, 0)),
            pl.BlockSpec((1, 1, nb, MOBA_VROWS, MOBA_BLOCK), lambda b, h, t: (b, h, 0, 0, 0)),
        ],
        out_specs=pl.BlockSpec((1, MOBA_HD, qt), lambda b, h, t: (b, h, t)),
        out_shape=jax.ShapeDtypeStruct((bsz, MOBA_W, t_len), BF16),
        scratch_shapes=[
            pltpu.VMEM((2, MOBA_KT, MOBA_QG), F32),
            pltpu.VMEM((2, MOBA_KT, MOBA_QG), BF16),
            pltpu.VMEM((qt // MOBA_QG, MOBA_VROWS, MOBA_QG), F32),
        ],
        compiler_params=pltpu.CompilerParams(
            dimension_semantics=("arbitrary", "arbitrary", "arbitrary"),
            vmem_limit_bytes=VMEM_LIMIT),
        name="moba_attn",
    )(qa, ka, vt)


def _merge_kernel(x_ref, gpre_ref, bra_ref, brbt_ref, brc_ref, wgate_ref,
                  wa_ref, wb_ref, wc_ref, wout_ref, gpost_ref, o_ref):
    x = x_ref[0]
    h = _rms(x, gpre_ref[...]).astype(BF16)
    merged = jax.nn.sigmoid(_dot(h, wgate_ref[:, 0:D_MODEL])) * _dot(bra_ref[0], wa_ref[...])
    merged = merged + (jax.nn.sigmoid(_dot(h, wgate_ref[:, D_MODEL:2 * D_MODEL]))
                       * _dot_tn(brbt_ref[0], wb_ref[...]))
    merged = merged + (jax.nn.sigmoid(_dot(h, wgate_ref[:, 2 * D_MODEL:3 * D_MODEL]))
                       * _dot(brc_ref[0], wc_ref[...]))
    y = _dot(merged.astype(BF16), wout_ref[...])
    o_ref[0] = x + _rms(y, gpost_ref[...])


def _merge(x, gpre, bra, brbt, brc, wgate, wa, wb, wc, wout, gpost, tm):
    bsz, t_len, _ = x.shape
    const = lambda b, t: (0, 0)
    return pl.pallas_call(
        _merge_kernel,
        grid=(bsz, t_len // tm),
        in_specs=[
            pl.BlockSpec((1, tm, D_MODEL), lambda b, t: (b, t, 0)),
            pl.BlockSpec((1, D_MODEL), const),
            pl.BlockSpec((1, tm, GLA_DV), lambda b, t: (b, t, 0)),
            pl.BlockSpec((1, MOBA_W, tm), lambda b, t: (b, 0, t)),
            pl.BlockSpec((1, tm, SC_W), lambda b, t: (b, t, 0)),
            pl.BlockSpec(wgate.shape, const),
            pl.BlockSpec(wa.shape, const),
            pl.BlockSpec(wb.shape, const),
            pl.BlockSpec(wc.shape, const),
            pl.BlockSpec(wout.shape, const),
            pl.BlockSpec((1, D_MODEL), const),
        ],
        out_specs=pl.BlockSpec((1, tm, D_MODEL), lambda b, t: (b, t, 0)),
        out_shape=jax.ShapeDtypeStruct(x.shape, x.dtype),
        compiler_params=pltpu.CompilerParams(
            dimension_semantics=("arbitrary", "arbitrary"),
            vmem_limit_bytes=VMEM_LIMIT),
        name="merge",
    )(x, gpre, bra, brbt, brc, wgate, wa, wb, wc, wout, gpost)


def _ffn_kernel(x_ref, gpre_ref, wup_ref, cw_ref, cb_ref, wdn_ref, gpost_ref,
                o_ref, halo_ref):
    tm = x_ref.shape[1]

    @pl.when(pl.program_id(1) == 0)
    def _():
        halo_ref[...] = jnp.zeros_like(halo_ref)

    x = x_ref[0]
    h = _rms(x, gpre_ref[...]).astype(BF16)
    y = jnp.zeros((tm, D_MODEL), F32)
    for ci in range(D_FF // FFN_CHUNK):
        halves = []
        for base in (ci * FFN_CHUNK, D_FF + ci * FFN_CHUNK):
            cols = slice(base, base + FFN_CHUNK)
            u = _dot(h, wup_ref[:, cols])
            conv = _causal_conv3(u, halo_ref[:, cols], cw_ref[:, cols]) + cb_ref[:, cols]
            halo_ref[:, cols] = u[tm - 8:tm, :]
            halves.append(conv)
        act = (halves[0] * jax.nn.sigmoid(halves[0]) * halves[1]).astype(BF16)
        y = y + _dot(act, wdn_ref[ci * FFN_CHUNK:(ci + 1) * FFN_CHUNK, :])
    o_ref[0] = x + _rms(y, gpost_ref[...])


def _ffn(x, gpre, wup, cw, cb, wdn, gpost, tm):
    bsz, t_len, _ = x.shape
    const = lambda b, t: (0, 0)
    return pl.pallas_call(
        _ffn_kernel,
        grid=(bsz, t_len // tm),
        in_specs=[
            pl.BlockSpec((1, tm, D_MODEL), lambda b, t: (b, t, 0)),
            pl.BlockSpec((1, D_MODEL), const),
            pl.BlockSpec(wup.shape, const, pipeline_mode=pl.Buffered(1)),
            pl.BlockSpec(cw.shape, const),
            pl.BlockSpec(cb.shape, const),
            pl.BlockSpec(wdn.shape, const, pipeline_mode=pl.Buffered(1)),
            pl.BlockSpec((1, D_MODEL), const),
        ],
        out_specs=pl.BlockSpec((1, tm, D_MODEL), lambda b, t: (b, t, 0)),
        out_shape=jax.ShapeDtypeStruct(x.shape, x.dtype),
        scratch_shapes=[pltpu.VMEM((8, 2 * D_FF), F32)],
        compiler_params=pltpu.CompilerParams(
            dimension_semantics=("arbitrary", "arbitrary"),
            vmem_limit_bytes=VMEM_LIMIT),
        name="ffn",
    )(x, gpre, wup, cw, cb, wdn, gpost)


def _pad_rows(w, rows):
    return jnp.concatenate([w, jnp.zeros((rows - w.shape[0],) + w.shape[1:], w.dtype)], axis=0)


def kernel(x, g_mix_pre, w_in, gla_w_lr2, gla_b_lr, gla_norm, sc_conv_w, w_br_gla, w_br_moba,
           w_br_sc, w_out, g_mix_post, g_ffn_pre, ffn_w_up, ffn_conv_w, ffn_conv_b, ffn_w_down,
           g_ffn_post):
    bsz, t_len, _ = x.shape
    depth = w_in.shape[0]
    tm = min(512, t_len)
    tg = min(256, t_len)
    nb = t_len // MOBA_BLOCK
    gla_cols = 2 * GLA_DK + 2 * GLA_DV + GLA_LOWRANK
    moba_end = gla_cols + 3 * MOBA_W
    sc_end = moba_end + 3 * SC_W

    for l in range(depth):
        w = w_in[l]
        wg = jnp.concatenate(
            [w[:, :gla_cols], jnp.zeros((D_MODEL, GLA_LR_PAD - GLA_LOWRANK), w.dtype)],
            axis=1).astype(BF16)
        wm = w[:, gla_cols:moba_end].astype(BF16)
        ws = w[:, moba_end:sc_end].astype(BF16)
        wgate = w[:, sc_end:].astype(BF16)
        row = lambda a: a.reshape(1, -1)

        gla_in, moba, br_c = _inproj(x, row(g_mix_pre[l]), wg, wm, ws,
                                     _pad_rows(sc_conv_w[l], 8), tm)
        br_a = _gla(gla_in, _pad_rows(gla_w_lr2[l], GLA_LR_PAD), row(gla_b_lr[l]),
                    row(jnp.tile(gla_norm[l], GLA_HEADS)), tg)
        km = _kmean(moba, min(2048, t_len))
        km = km.reshape(bsz, nb, MOBA_HEADS, MOBA_HD).transpose(0, 2, 1, 3)
        km = jnp.pad(km, ((0, 0), (0, 0), (0, MOBA_MAXB - nb), (0, 0)))
        qa, ka, vt = _moba_prep(moba, km)
        br_bt = _moba_attn(qa, ka, vt)
        x = _merge(x, row(g_mix_pre[l]), br_a, br_bt, br_c, wgate,
                   w_br_gla[l].astype(BF16), w_br_moba[l].astype(BF16),
                   w_br_sc[l].astype(BF16), w_out[l].astype(BF16), row(g_mix_post[l]), tm)
        x = _ffn(x, row(g_ffn_pre[l]), ffn_w_up[l].astype(BF16), _pad_rows(ffn_conv_w[l], 8),
                 row(ffn_conv_b[l]), ffn_w_down[l].astype(BF16), row(g_ffn_post[l]), tm)
    return x
```

```python
import functools

import jax
import jax.numpy as jnp
from jax import lax
from jax.experimental import pallas as pl
from jax.experimental.pallas import tpu as pltpu

F32 = jnp.float32
BF16 = jnp.bfloat16

D_MODEL = 1024
EPS = 1e-6

GLA_HEADS = 4
GLA_DK = 256
GLA_DV = 512
GLA_HK = GLA_DK // GLA_HEADS
GLA_HV = GLA_DV // GLA_HEADS
GLA_LOWRANK = 16
GLA_TAU = 16.0
GLA_CHUNK = 64
GLA_SUB = 8
GLA_LR_PAD = 128
GLA_IN = 2 * GLA_DK + 2 * GLA_DV + GLA_LR_PAD

MOBA_HEADS = 8
MOBA_HD = 64
MOBA_W = MOBA_HEADS * MOBA_HD
MOBA_BLOCK = 256
MOBA_TOPK = 3
MOBA_AUG = 2 * MOBA_HD
MOBA_MAXB = MOBA_AUG - MOBA_HD
MOBA_NEG = -65536.0
MOBA_QT = 2048
MOBA_KT = 512
MOBA_QG = 256
MOBA_SBUF = 4
MOBA_AHEAD = 3
MOBA_VROWS = MOBA_HD + 16

SC_W = 512
D_FF = 2816
FFN_CHUNK = 1408

VMEM_LIMIT = 56 * 1024 * 1024


def _rms(x, g):
    return x * lax.rsqrt(jnp.mean(x * x, axis=-1, keepdims=True) + EPS) * g


def _dot(a, b):
    return jnp.dot(a, b, preferred_element_type=F32)


def _dot_nt(a, b):
    return lax.dot_general(a, b, (((1,), (1,)), ((), ())), preferred_element_type=F32)


def _dot_tn(a, b):
    return lax.dot_general(a, b, (((0,), (0,)), ((), ())), preferred_element_type=F32)


def _split3(a):
    hi = a.astype(BF16)
    r1 = a - hi.astype(F32)
    mid = r1.astype(BF16)
    lo = (r1 - mid.astype(F32)).astype(BF16)
    return hi, mid, lo


def _causal_conv3(p, halo, w):
    row = lax.broadcasted_iota(jnp.int32, p.shape, 0)
    p1 = pltpu.roll(p, 1, 0)
    p1 = jnp.where(row == 0, halo[7:8, :], p1)
    p2 = pltpu.roll(p, 2, 0)
    p2 = jnp.where(row == 0, halo[6:7, :], jnp.where(row == 1, halo[7:8, :], p2))
    return w[0:1, :] * p2 + w[1:2, :] * p1 + w[2:3, :] * p


def _inproj_kernel(x_ref, g_ref, wg_ref, wm_ref, ws_ref, cw_ref,
                   gla_ref, moba_ref, brc_ref, halo_ref):
    tm = x_ref.shape[1]

    @pl.when(pl.program_id(1) == 0)
    def _():
        halo_ref[...] = jnp.zeros_like(halo_ref)

    h = _rms(x_ref[0], g_ref[...]).astype(BF16)
    gla_ref[0] = _dot(h, wg_ref[...])
    moba_ref[0] = _dot(h, wm_ref[...])
    s = _dot(h, ws_ref[...])
    sb = s[:, 0:SC_W]
    p = s[:, SC_W:2 * SC_W] * s[:, 2 * SC_W:3 * SC_W]
    y = _causal_conv3(p, halo_ref[...], cw_ref[...])
    halo_ref[...] = p[tm - 8:tm, :]
    brc_ref[0] = (sb * y).astype(BF16)


def _inproj(x, g, wg, wm, ws, cw, tm):
    bsz, t_len, _ = x.shape
    const = lambda b, t: (0, 0)
    return pl.pallas_call(
        _inproj_kernel,
        grid=(bsz, t_len // tm),
        in_specs=[
            pl.BlockSpec((1, tm, D_MODEL), lambda b, t: (b, t, 0)),
            pl.BlockSpec((1, D_MODEL), const),
            pl.BlockSpec(wg.shape, const),
            pl.BlockSpec(wm.shape, const),
            pl.BlockSpec(ws.shape, const),
            pl.BlockSpec(cw.shape, const),
        ],
        out_specs=[
            pl.BlockSpec((1, tm, GLA_IN), lambda b, t: (b, t, 0)),
            pl.BlockSpec((1, tm, 3 * MOBA_W), lambda b, t: (b, t, 0)),
            pl.BlockSpec((1, tm, SC_W), lambda b, t: (b, t, 0)),
        ],
        out_shape=[
            jax.ShapeDtypeStruct((bsz, t_len, GLA_IN), F32),
            jax.ShapeDtypeStruct((bsz, t_len, 3 * MOBA_W), F32),
            jax.ShapeDtypeStruct((bsz, t_len, SC_W), BF16),
        ],
        scratch_shapes=[pltpu.VMEM((8, SC_W), F32)],
        compiler_params=pltpu.CompilerParams(
            dimension_semantics=("arbitrary", "arbitrary"),
            vmem_limit_bytes=VMEM_LIMIT),
        name="inproj",
    )(x, g, wg, wm, ws, cw)


def _gla_kernel(z_ref, wlr_ref, blr_ref, gn_ref, o_ref, st_ref):
    tg = z_ref.shape[1]
    c = GLA_CHUNK

    @pl.when(pl.program_id(0) == 0)
    def _():
        st_ref[...] = jnp.zeros_like(st_ref)

    row = lax.broadcasted_iota(jnp.int32, (c, GLA_DK), 0)
    col_j = lax.broadcasted_iota(jnp.int32, (c, GLA_HEADS * c), 1) % c
    prow = lax.broadcasted_iota(jnp.int32, (c, GLA_HEADS * c), 0)
    arow = lax.broadcasted_iota(jnp.int32, (c, c), 0)
    acol = lax.broadcasted_iota(jnp.int32, (c, c), 1)
    tril = (arow >= acol).astype(BF16)
    ed = lax.broadcasted_iota(jnp.int32, (GLA_DK, GLA_HEADS * c), 0) // GLA_HK
    ej = lax.broadcasted_iota(jnp.int32, (GLA_DK, GLA_HEADS * c), 1) // c
    expand = (ed == ej).astype(BF16)
    kh = lax.broadcasted_iota(jnp.int32, (GLA_HEADS * c, GLA_DK), 0) // c
    kd_head = lax.broadcasted_iota(jnp.int32, (GLA_HEADS * c, GLA_DK), 1) // GLA_HK
    key_mask = kh == kd_head
    vh = lax.broadcasted_iota(jnp.int32, (GLA_HEADS * c, GLA_DV), 0) // c
    ve_head = lax.broadcasted_iota(jnp.int32, (GLA_HEADS * c, GLA_DV), 1) // GLA_HV
    val_mask = vh == ve_head
    se = lax.broadcasted_iota(jnp.int32, (GLA_DV, GLA_DK), 0) // GLA_HV
    sd = lax.broadcasted_iota(jnp.int32, (GLA_DV, GLA_DK), 1) // GLA_HK
    same_head = se == sd
    neg_inf = jnp.float32(-jnp.inf)
    w1, w2, w3 = _split3(wlr_ref[...])

    def one_batch(bi, rows):
        q = z_ref[bi, rows, 0:GLA_DK] * (GLA_HK ** -0.5)
        k = z_ref[bi, rows, GLA_DK:2 * GLA_DK]
        v = z_ref[bi, rows, 2 * GLA_DK:2 * GLA_DK + GLA_DV]
        r = z_ref[bi, rows, 2 * GLA_DK + GLA_DV:2 * GLA_DK + 2 * GLA_DV]
        lr = z_ref[bi, rows, 2 * GLA_DK + 2 * GLA_DV:GLA_IN]

        l1, l2, l3 = _split3(lr)
        zl = (_dot(l1, w1) + (_dot(l1, w2) + _dot(l2, w1))
              + (_dot(l1, w3) + _dot(l2, w2) + _dot(l3, w1))) + blr_ref[...]
        yield
        log_a = (jnp.minimum(zl, 0.0) - jnp.log(1.0 + jnp.exp(-jnp.abs(zl)))) * (1.0 / GLA_TAU)
        a1, a2, a3 = _split3(log_a)
        bc = _dot(tril, a1) + _dot(tril, a2) + _dot(tril, a3)
        yield

        att = jnp.zeros((c, GLA_HEADS * c), F32)
        s = c // 2
        while s >= GLA_SUB:
            ngrp = c // (2 * s)
            ref = jnp.concatenate(
                [jnp.broadcast_to(bc[g * 2 * s + s:g * 2 * s + s + 1, :], (2 * s, GLA_DK))
                 for g in range(ngrp)], axis=0)
            later = (row % (2 * s)) >= s
            ql = (q * jnp.exp(jnp.where(later, bc - ref, neg_inf))).astype(BF16)
            kl = (k * jnp.exp(jnp.where(later, neg_inf, ref - bc))).astype(BF16)
            kl_heads = jnp.where(key_mask, jnp.concatenate([kl] * GLA_HEADS, axis=0),
                                 jnp.zeros((), BF16))
            part = _dot_nt(ql, kl_heads)
            yield
            if ngrp > 1:
                part = jnp.where((prow // (2 * s)) == (col_j // (2 * s)), part, 0.0)
            att = att + part
            s //= 2

        prods = []
        for dlt in range(GLA_SUB):
            if dlt == 0:
                prods.append((q * k).astype(BF16))
            else:
                ok = (row % GLA_SUB) >= dlt
                kd = pltpu.roll(k, dlt, 0)
                bd = pltpu.roll(bc, dlt, 0)
                prods.append((q * kd * jnp.exp(jnp.where(ok, bc - bd, neg_inf))).astype(BF16))
        coef = _dot(jnp.concatenate(prods, axis=0), expand)
        yield
        for dlt in range(GLA_SUB):
            att = att + jnp.where(col_j == prow - dlt, coef[dlt * c:(dlt + 1) * c, :], 0.0)

        vb = v.astype(BF16)
        v_heads = jnp.where(val_mask, jnp.concatenate([vb] * GLA_HEADS, axis=0),
                            jnp.zeros((), BF16))
        o = _dot(att.astype(BF16), v_heads)
        yield
        st = st_ref[bi]
        o_state = _dot_nt((q * jnp.exp(bc)).astype(BF16), st.astype(BF16))
        yield
        o = o + o_state

        blast = bc[c - 1:c, :]
        kdec = (k * jnp.exp(blast - bc)).astype(BF16)
        upd = _dot_tn(vb, kdec)
        yield
        st_ref[bi] = st * jnp.exp(blast) + jnp.where(same_head, upd, 0.0)

        outs = []
        for hh in range(GLA_HEADS):
            oh = o[:, hh * GLA_HV:(hh + 1) * GLA_HV]
            outs.append(oh * lax.rsqrt(jnp.mean(oh * oh, axis=-1, keepdims=True) + EPS))
        on = jnp.concatenate(outs, axis=1) * gn_ref[...]
        o_ref[bi, rows, :] = (on * (r * jax.nn.sigmoid(r))).astype(o_ref.dtype)

    def chunk(ci, carry):
        rows = pl.ds(pl.multiple_of(ci * c, c), c)
        live = [one_batch(bi, rows) for bi in range(z_ref.shape[0])]
        while live:
            live = [gen for gen in live if next(gen, "done") != "done"]
        return carry

    lax.fori_loop(0, tg // c, chunk, 0)


def _gla(z, wlr, blr, gn, tg):
    bsz, t_len, _ = z.shape
    const = lambda t: (0, 0)
    return pl.pallas_call(
        _gla_kernel,
        grid=(t_len // tg,),
        in_specs=[
            pl.BlockSpec((bsz, tg, GLA_IN), lambda t: (0, t, 0)),
            pl.BlockSpec(wlr.shape, const),
            pl.BlockSpec(blr.shape, const),
            pl.BlockSpec(gn.shape, const),
        ],
        out_specs=pl.BlockSpec((bsz, tg, GLA_DV), lambda t: (0, t, 0)),
        out_shape=jax.ShapeDtypeStruct((bsz, t_len, GLA_DV), BF16),
        scratch_shapes=[pltpu.VMEM((bsz, GLA_DV, GLA_DK), F32)],
        compiler_params=pltpu.CompilerParams(
            dimension_semantics=("arbitrary",),
            vmem_limit_bytes=VMEM_LIMIT),
        name="gla",
    )(z, wlr, blr, gn)


def _kmean_kernel(k_ref, o_ref):
    nblk = k_ref.shape[1] // MOBA_BLOCK
    k = k_ref[0].reshape(nblk, MOBA_BLOCK, MOBA_W)
    o_ref[0] = jnp.sum(k, axis=1) * (1.0 / MOBA_BLOCK)


def _kmean(moba, rows):
    bsz, t_len, _ = moba.shape
    return pl.pallas_call(
        _kmean_kernel,
        grid=(bsz, t_len // rows),
        in_specs=[pl.BlockSpec((1, rows, MOBA_W), lambda b, t: (b, t, 1))],
        out_specs=pl.BlockSpec((1, rows // MOBA_BLOCK, MOBA_W), lambda b, t: (b, t, 0)),
        out_shape=jax.ShapeDtypeStruct((bsz, t_len // MOBA_BLOCK, MOBA_W), F32),
        compiler_params=pltpu.CompilerParams(
            dimension_semantics=("arbitrary", "arbitrary"),
            vmem_limit_bytes=VMEM_LIMIT),
        name="moba_kmean",
    )(moba)


def _moba_prep_kernel(m_ref, km_ref, qa_ref, ka_ref, vt_ref):
    nb = MOBA_MAXB
    n_own = pl.program_id(1)
    n_own_f = n_own.astype(F32)
    q = m_ref[0, :, 0:MOBA_W]
    k = m_ref[0, :, MOBA_W:2 * MOBA_W]
    v = m_ref[0, :, 2 * MOBA_W:3 * MOBA_W]
    qt = q.T * (MOBA_HD ** -0.5)
    vt_ref[0, :, 0, 0:MOBA_HD, :] = v.T.reshape(MOBA_HEADS, MOBA_HD, MOBA_BLOCK).astype(BF16)
    tail = lax.broadcasted_iota(jnp.int32, (MOBA_HEADS, MOBA_VROWS - MOBA_HD, MOBA_BLOCK), 1)
    vt_ref[0, :, 0, MOBA_HD:MOBA_VROWS, :] = (tail == 0).astype(BF16)

    brow = lax.broadcasted_iota(jnp.int32, (nb, MOBA_BLOCK), 0).astype(F32)
    neg_inf = jnp.float32(-jnp.inf)
    lane = lax.broadcasted_iota(jnp.int32, (MOBA_BLOCK, MOBA_AUG), 1)
    pos = lax.broadcasted_iota(jnp.int32, (MOBA_BLOCK, MOBA_AUG), 0).astype(F32)

    for hh in range(MOBA_HEADS):
        slope = 2.0 ** (-8.0 * (hh + 1) / MOBA_HEADS)
        qth = qt[hh * MOBA_HD:(hh + 1) * MOBA_HD, :]
        q1, q2, q3 = _split3(qth)
        m1, m2, m3 = _split3(km_ref[0, hh])
        gate = (_dot(m1, q1) + (_dot(m1, q2) + _dot(m2, q1))
                + (_dot(m1, q3) + _dot(m2, q2) + _dot(m3, q1)))
        gate = jnp.where(brow < n_own_f, gate, neg_inf)
        chosen = jnp.zeros((nb, MOBA_BLOCK), jnp.bool_)
        for _ in range(MOBA_TOPK):
            best = jnp.max(gate, axis=0, keepdims=True)
            first = jnp.min(jnp.where(gate == best, brow, float(nb)), axis=0, keepdims=True)
            hit = brow == first
            chosen = chosen | (hit & (best > neg_inf))
            gate = jnp.where(hit, neg_inf, gate)
        term = jnp.where(chosen, (slope * MOBA_BLOCK) * (brow - n_own_f), MOBA_NEG)
        term = jnp.where(brow == n_own_f, 0.0, term)
        term = jnp.where(brow == float(nb - 1), 1.0, term)
        halves = [qth, term] if hh % 2 == 0 else [term, qth]
        qa_ref[0, hh] = jnp.concatenate(halves, axis=0).astype(BF16)

        kpair = k[:, (hh // 2) * MOBA_AUG:(hh // 2 + 1) * MOBA_AUG]
        idx = lane - MOBA_HD if hh % 2 == 0 else lane
        extra = jnp.where(idx == nb - 1, slope * pos, (idx == n_own).astype(F32))
        keep = (lane < MOBA_HD) if hh % 2 == 0 else (lane >= MOBA_HD)
        ka_ref[0, hh] = jnp.where(keep, kpair, extra).astype(BF16)


def _moba_prep(moba, km):
    bsz, t_len, _ = moba.shape
    nb = t_len // MOBA_BLOCK
    return pl.pallas_call(
        _moba_prep_kernel,
        grid=(bsz, nb),
        in_specs=[
            pl.BlockSpec((1, MOBA_BLOCK, 3 * MOBA_W), lambda b, t: (b, t, 0)),
            pl.BlockSpec((1, MOBA_HEADS, MOBA_MAXB, MOBA_HD), lambda b, t: (b, 0, 0, 0)),
        ],
        out_specs=[
            pl.BlockSpec((1, MOBA_HEADS, MOBA_AUG, MOBA_BLOCK), lambda b, t: (b, 0, 0, t)),
            pl.BlockSpec((1, MOBA_HEADS, MOBA_BLOCK, MOBA_AUG), lambda b, t: (b, 0, t, 0)),
            pl.BlockSpec((1, MOBA_HEADS, 1, MOBA_VROWS, MOBA_BLOCK), lambda b, t: (b, 0, t, 0, 0)),
        ],
        out_shape=[
            jax.ShapeDtypeStruct((bsz, MOBA_HEADS, MOBA_AUG, t_len), BF16),
            jax.ShapeDtypeStruct((bsz, MOBA_HEADS, t_len, MOBA_AUG), BF16),
            jax.ShapeDtypeStruct((bsz, MOBA_HEADS, nb, MOBA_VROWS, MOBA_BLOCK), BF16),
        ],
        compiler_params=pltpu.CompilerParams(
            dimension_semantics=("arbitrary", "arbitrary"),
            vmem_limit_bytes=VMEM_LIMIT),
        name="moba_prep",
    )(moba, km)


def _moba_attn_kernel(qa_ref, ka_ref, vt_ref, o_ref, s_scr, p_scr, acc_scr):
    j = pl.program_id(2)
    qt = qa_ref.shape[3]
    ngrp = qt // MOBA_QG
    kb = MOBA_KT // MOBA_BLOCK
    dsteps = qt // MOBA_KT
    base = j * dsteps

    def keys(step):
        r0 = pl.multiple_of(step * MOBA_KT, MOBA_KT)
        return ka_ref[0, 0, pl.ds(r0, MOBA_KT), :]

    def score_stage(buf, step, g, mask_from=None):
        s = _dot(keys(step), qa_ref[0, 0, :, g * MOBA_QG:(g + 1) * MOBA_QG])
        if mask_from is not None:
            krow = lax.broadcasted_iota(jnp.int32, s.shape, 0) + mask_from
            qcol = lax.broadcasted_iota(jnp.int32, s.shape, 1) + g * MOBA_QG
            s = jnp.where(krow <= qcol, s, -jnp.inf)
        s_scr[buf] = s

    def softmax_stage(buf, pbuf, m):
        s = s_scr[buf]
        m_new = jnp.max(s, axis=0, keepdims=True)
        alpha = None
        if m is not None:
            m_new = jnp.maximum(m, m_new)
            alpha = jnp.exp(m - m_new)
        p_scr[pbuf] = jnp.exp(s - m_new).astype(BF16)
        return m_new, alpha

    def value_stage(buf, g, alpha, step):
        pv = _dot(vt_ref[0, 0, step * kb], p_scr[buf, 0:MOBA_BLOCK, :])
        for i in range(1, kb):
            pv = pv + _dot(vt_ref[0, 0, step * kb + i],
                           p_scr[buf, i * MOBA_BLOCK:(i + 1) * MOBA_BLOCK, :])
        acc_scr[g] = pv if alpha is None else alpha * acc_scr[g] + pv

    items = []
    for d in range(dsteps):
        k_lo, k_hi = d * MOBA_KT, (d + 1) * MOBA_KT - 1
        for g in range(ngrp):
            q_lo, q_hi = g * MOBA_QG, (g + 1) * MOBA_QG - 1
            if k_lo <= q_hi:
                items.append((d, g, k_lo if k_hi > q_lo else None))
    nbuf, ahead = s_scr.shape[0], MOBA_AHEAD
    assert ngrp % nbuf == 0 and ngrp % 2 == 0 and len(items) % 2 == 0 and ahead < nbuf
    assert items[-1][1] == ngrp - 1 and ahead <= ngrp
    par0 = len(items) % nbuf

    def diag_score(i):
        if i < len(items):
            score_stage(i % nbuf, base + items[i][0], items[i][1], items[i][2])
        else:
            score_stage(i % nbuf, 0, i - len(items))

    ms = [None] * ngrp
    for i in range(ahead):
        diag_score(i)
    pend = None
    for i, (d, g, _) in enumerate(items):
        diag_score(i + ahead)
        ms[g], alpha = softmax_stage(i % nbuf, i % 2, ms[g])
        if pend is not None:
            value_stage((i - 1) % 2, *pend)
        pend = (g, alpha, base + d)

    def body(n, carry):
        ms, alpha_p, step_p = carry
        ms = list(ms)
        pend = (ngrp - 1, alpha_p, step_p)
        for g in range(ngrp):
            if g + ahead < ngrp:
                score_stage((par0 + g + ahead) % nbuf, n, g + ahead)
            else:
                score_stage((par0 + g + ahead) % nbuf, jnp.minimum(n + 1, base - 1),
                            g + ahead - ngrp)
            ms[g], alpha = softmax_stage((par0 + g) % nbuf, g % 2, ms[g])
            value_stage((g - 1) % 2, *pend)
            pend = (g, alpha, n)
        return tuple(ms), pend[1], pend[2]

    _, alpha_p, step_p = lax.fori_loop(0, base, body, (tuple(ms), pend[1], pend[2]))
    value_stage(1, ngrp - 1, alpha_p, step_p)
    for g in range(ngrp):
        num, den = acc_scr[g, 0:MOBA_HD, :], acc_scr[g, MOBA_HD:MOBA_HD + 1, :]
        o_ref[0, :, g * MOBA_QG:(g + 1) * MOBA_QG] = (num / den).astype(o_ref.dtype)


def _moba_attn(qa, ka, vt):
    bsz, _, _, t_len = qa.shape
    nb = t_len // MOBA_BLOCK
    qt = min(MOBA_QT, t_len)
    return pl.pallas_call(
        _moba_attn_kernel,
        grid=(bsz, MOBA_HEADS, t_len // qt),
        in_specs=[
            pl.BlockSpec((1, 1, MOBA_AUG, qt), lambda b, h, t: (b, h, 0, t)),
            pl.BlockSpec((1, 1, t_len, MOBA_AUG), lambda b, h, t: (b, h, 0, 0)),
            pl.BlockSpec((1, 1, nb, MOBA_VROWS, MOBA_BLOCK), lambda b, h, t: (b, h, 0, 0, 0)),
        ],
        out_specs=pl.BlockSpec((1, MOBA_HD, qt), lambda b, h, t: (b, h, t)),
        out_shape=jax.ShapeDtypeStruct((bsz, MOBA_W, t_len), BF16),
        scratch_shapes=[
            pltpu.VMEM((MOBA_SBUF, MOBA_KT, MOBA_QG), F32),
            pltpu.VMEM((2, MOBA_KT, MOBA_QG), BF16),
            pltpu.VMEM((qt // MOBA_QG, MOBA_VROWS, MOBA_QG), F32),
        ],
        compiler_params=pltpu.CompilerParams(
            dimension_semantics=("arbitrary", "arbitrary", "arbitrary"),
            vmem_limit_bytes=VMEM_LIMIT),
        name="moba_attn",
    )(qa, ka, vt)


def _merge_kernel(x_ref, gpre_ref, bra_ref, brbt_ref, brc_ref, wgate_ref,
                  wa_ref, wb_ref, wc_ref, wout_ref, gpost_ref, o_ref):
    x = x_ref[0]
    h = _rms(x, gpre_ref[...]).astype(BF16)
    merged = jax.nn.sigmoid(_dot(h, wgate_ref[:, 0:D_MODEL])) * _dot(bra_ref[0], wa_ref[...])
    merged = merged + (jax.nn.sigmoid(_dot(h, wgate_ref[:, D_MODEL:2 * D_MODEL]))
                       * _dot_tn(brbt_ref[0], wb_ref[...]))
    merged = merged + (jax.nn.sigmoid(_dot(h, wgate_ref[:, 2 * D_MODEL:3 * D_MODEL]))
                       * _dot(brc_ref[0], wc_ref[...]))
    y = _dot(merged.astype(BF16), wout_ref[...])
    o_ref[0] = x + _rms(y, gpost_ref[...])


def _merge(x, gpre, bra, brbt, brc, wgate, wa, wb, wc, wout, gpost, tm):
    bsz, t_len, _ = x.shape
    const = lambda b, t: (0, 0)
    return pl.pallas_call(
        _merge_kernel,
        grid=(bsz, t_len // tm),
        in_specs=[
            pl.BlockSpec((1, tm, D_MODEL), lambda b, t: (b, t, 0)),
            pl.BlockSpec((1, D_MODEL), const),
            pl.BlockSpec((1, tm, GLA_DV), lambda b, t: (b, t, 0)),
            pl.BlockSpec((1, MOBA_W, tm), lambda b, t: (b, 0, t)),
            pl.BlockSpec((1, tm, SC_W), lambda b, t: (b, t, 0)),
            pl.BlockSpec(wgate.shape, const),
            pl.BlockSpec(wa.shape, const),
            pl.BlockSpec(wb.shape, const),
            pl.BlockSpec(wc.shape, const),
            pl.BlockSpec(wout.shape, const),
            pl.BlockSpec((1, D_MODEL), const),
        ],
        out_specs=pl.BlockSpec((1, tm, D_MODEL), lambda b, t: (b, t, 0)),
        out_shape=jax.ShapeDtypeStruct(x.shape, x.dtype),
        compiler_params=pltpu.CompilerParams(
            dimension_semantics=("arbitrary", "arbitrary"),
            vmem_limit_bytes=VMEM_LIMIT),
        name="merge",
    )(x, gpre, bra, brbt, brc, wgate, wa, wb, wc, wout, gpost)


def _ffn_kernel(x_ref, gpre_ref, wup_ref, cw_ref, cb_ref, wdn_ref, gpost_ref,
                o_ref, halo_ref):
    tm = x_ref.shape[1]

    @pl.when(pl.program_id(1) == 0)
    def _():
        halo_ref[...] = jnp.zeros_like(halo_ref)

    x = x_ref[0]
    h = _rms(x, gpre_ref[...]).astype(BF16)
    y = jnp.zeros((tm, D_MODEL), F32)
    for ci in range(D_FF // FFN_CHUNK):
        halves = []
        for base in (ci * FFN_CHUNK, D_FF + ci * FFN_CHUNK):
            cols = slice(base, base + FFN_CHUNK)
            u = _dot(h, wup_ref[:, cols])
            conv = _causal_conv3(u, halo_ref[:, cols], cw_ref[:, cols]) + cb_ref[:, cols]
            halo_ref[:, cols] = u[tm - 8:tm, :]
            halves.append(conv)
        act = (halves[0] * jax.nn.sigmoid(halves[0]) * halves[1]).astype(BF16)
        y = y + _dot(act, wdn_ref[ci * FFN_CHUNK:(ci + 1) * FFN_CHUNK, :])
    o_ref[0] = x + _rms(y, gpost_ref[...])


def _ffn(x, gpre, wup, cw, cb, wdn, gpost, tm):
    bsz, t_len, _ = x.shape
    const = lambda b, t: (0, 0)
    return pl.pallas_call(
        _ffn_kernel,
        grid=(bsz, t_len // tm),
        in_specs=[
            pl.BlockSpec((1, tm, D_MODEL), lambda b, t: (b, t, 0)),
            pl.BlockSpec((1, D_MODEL), const),
            pl.BlockSpec(wup.shape, const, pipeline_mode=pl.Buffered(1)),
            pl.BlockSpec(cw.shape, const),
            pl.BlockSpec(cb.shape, const),
            pl.BlockSpec(wdn.shape, const, pipeline_mode=pl.Buffered(1)),
            pl.BlockSpec((1, D_MODEL), const),
        ],
        out_specs=pl.BlockSpec((1, tm, D_MODEL), lambda b, t: (b, t, 0)),
        out_shape=jax.ShapeDtypeStruct(x.shape, x.dtype),
        scratch_shapes=[pltpu.VMEM((8, 2 * D_FF), F32)],
        compiler_params=pltpu.CompilerParams(
            dimension_semantics=("arbitrary", "arbitrary"),
            vmem_limit_bytes=VMEM_LIMIT),
        name="ffn",
    )(x, gpre, wup, cw, cb, wdn, gpost)


def _pad_rows(w, rows):
    return jnp.concatenate([w, jnp.zeros((rows - w.shape[0],) + w.shape[1:], w.dtype)], axis=0)


def kernel(x, g_mix_pre, w_in, gla_w_lr2, gla_b_lr, gla_norm, sc_conv_w, w_br_gla, w_br_moba,
           w_br_sc, w_out, g_mix_post, g_ffn_pre, ffn_w_up, ffn_conv_w, ffn_conv_b, ffn_w_down,
           g_ffn_post):
    bsz, t_len, _ = x.shape
    depth = w_in.shape[0]
    tm = min(512, t_len)
    tg = min(256, t_len)
    nb = t_len // MOBA_BLOCK
    gla_cols = 2 * GLA_DK + 2 * GLA_DV + GLA_LOWRANK
    moba_end = gla_cols + 3 * MOBA_W
    sc_end = moba_end + 3 * SC_W

    for l in range(depth):
        w = w_in[l]
        wg = jnp.concatenate(
            [w[:, :gla_cols], jnp.zeros((D_MODEL, GLA_LR_PAD - GLA_LOWRANK), w.dtype)],
            axis=1).astype(BF16)
        wm = w[:, gla_cols:moba_end].astype(BF16)
        ws = w[:, moba_end:sc_end].astype(BF16)
        wgate = w[:, sc_end:].astype(BF16)
        row = lambda a: a.reshape(1, -1)

        gla_in, moba, br_c = _inproj(x, row(g_mix_pre[l]), wg, wm, ws,
                                     _pad_rows(sc_conv_w[l], 8), tm)
        br_a = _gla(gla_in, _pad_rows(gla_w_lr2[l], GLA_LR_PAD), row(gla_b_lr[l]),
                    row(jnp.tile(gla_norm[l], GLA_HEADS)), tg)
        km = _kmean(moba, min(2048, t_len))
        km = km.reshape(bsz, nb, MOBA_HEADS, MOBA_HD).transpose(0, 2, 1, 3)
        km = jnp.pad(km, ((0, 0), (0, 0), (0, MOBA_MAXB - nb), (0, 0)))
        qa, ka, vt = _moba_prep(moba, km)
        br_bt = _moba_attn(qa, ka, vt)
        x = _merge(x, row(g_mix_pre[l]), br_a, br_bt, br_c, wgate,
                   w_br_gla[l].astype(BF16), w_br_moba[l].astype(BF16),
                   w_br_sc[l].astype(BF16), w_out[l].astype(BF16), row(g_mix_post[l]), tm)
        x = _ffn(x, row(g_ffn_pre[l]), ffn_w_up[l].astype(BF16), _pad_rows(ffn_conv_w[l], 8),
                 row(ffn_conv_b[l]), ffn_w_down[l].astype(BF16), row(g_ffn_post[l]), tm)
    return x
```

```python
import functools

import jax
import jax.numpy as jnp
from jax import lax
from jax.experimental import pallas as pl
from jax.experimental.pallas import tpu as pltpu

F32 = jnp.float32
BF16 = jnp.bfloat16

D_MODEL = 1024
EPS = 1e-6

GLA_HEADS = 4
GLA_DK = 256
GLA_DV = 512
GLA_HK = GLA_DK // GLA_HEADS
GLA_HV = GLA_DV // GLA_HEADS
GLA_LOWRANK = 16
GLA_TAU = 16.0
GLA_CHUNK = 64
GLA_SUB = 8
GLA_LR_PAD = 128
GLA_IN = 2 * GLA_DK + 2 * GLA_DV + GLA_LR_PAD

MOBA_HEADS = 8
MOBA_HD = 64
MOBA_W = MOBA_HEADS * MOBA_HD
MOBA_BLOCK = 256
MOBA_TOPK = 3
MOBA_AUG = 2 * MOBA_HD
MOBA_MAXB = MOBA_AUG - MOBA_HD
MOBA_NEG = -65536.0
MOBA_QT = 4096
MOBA_KT = 512
MOBA_QG = 256
MOBA_SBUF = 4
MOBA_AHEAD = 3
MOBA_VROWS = MOBA_HD + 16

SC_W = 512
D_FF = 2816
FFN_CHUNK = 1408

VMEM_LIMIT = 56 * 1024 * 1024


def _rms(x, g):
    return x * lax.rsqrt(jnp.mean(x * x, axis=-1, keepdims=True) + EPS) * g


def _dot(a, b):
    return jnp.dot(a, b, preferred_element_type=F32)


def _dot_nt(a, b):
    return lax.dot_general(a, b, (((1,), (1,)), ((), ())), preferred_element_type=F32)


def _dot_tn(a, b):
    return lax.dot_general(a, b, (((0,), (0,)), ((), ())), preferred_element_type=F32)


def _split3(a):
    hi = a.astype(BF16)
    r1 = a - hi.astype(F32)
    mid = r1.astype(BF16)
    lo = (r1 - mid.astype(F32)).astype(BF16)
    return hi, mid, lo


def _causal_conv3(p, halo, w):
    row = lax.broadcasted_iota(jnp.int32, p.shape, 0)
    p1 = pltpu.roll(p, 1, 0)
    p1 = jnp.where(row == 0, halo[7:8, :], p1)
    p2 = pltpu.roll(p, 2, 0)
    p2 = jnp.where(row == 0, halo[6:7, :], jnp.where(row == 1, halo[7:8, :], p2))
    return w[0:1, :] * p2 + w[1:2, :] * p1 + w[2:3, :] * p


def _inproj_kernel(x_ref, g_ref, wg_ref, wm_ref, ws_ref, cw_ref,
                   gla_ref, moba_ref, brc_ref, halo_ref):
    tm = x_ref.shape[1]

    @pl.when(pl.program_id(1) == 0)
    def _():
        halo_ref[...] = jnp.zeros_like(halo_ref)

    h = _rms(x_ref[0], g_ref[...]).astype(BF16)
    gla_ref[0] = _dot(h, wg_ref[...])
    moba_ref[0] = _dot(h, wm_ref[...])
    s = _dot(h, ws_ref[...])
    sb = s[:, 0:SC_W]
    p = s[:, SC_W:2 * SC_W] * s[:, 2 * SC_W:3 * SC_W]
    y = _causal_conv3(p, halo_ref[...], cw_ref[...])
    halo_ref[...] = p[tm - 8:tm, :]
    brc_ref[0] = (sb * y).astype(BF16)


def _inproj(x, g, wg, wm, ws, cw, tm):
    bsz, t_len, _ = x.shape
    const = lambda b, t: (0, 0)
    return pl.pallas_call(
        _inproj_kernel,
        grid=(bsz, t_len // tm),
        in_specs=[
            pl.BlockSpec((1, tm, D_MODEL), lambda b, t: (b, t, 0)),
            pl.BlockSpec((1, D_MODEL), const),
            pl.BlockSpec(wg.shape, const),
            pl.BlockSpec(wm.shape, const),
            pl.BlockSpec(ws.shape, const),
            pl.BlockSpec(cw.shape, const),
        ],
        out_specs=[
            pl.BlockSpec((1, tm, GLA_IN), lambda b, t: (b, t, 0)),
            pl.BlockSpec((1, tm, 3 * MOBA_W), lambda b, t: (b, t, 0)),
            pl.BlockSpec((1, tm, SC_W), lambda b, t: (b, t, 0)),
        ],
        out_shape=[
            jax.ShapeDtypeStruct((bsz, t_len, GLA_IN), F32),
            jax.ShapeDtypeStruct((bsz, t_len, 3 * MOBA_W), F32),
            jax.ShapeDtypeStruct((bsz, t_len, SC_W), BF16),
        ],
        scratch_shapes=[pltpu.VMEM((8, SC_W), F32)],
        compiler_params=pltpu.CompilerParams(
            dimension_semantics=("arbitrary", "arbitrary"),
            vmem_limit_bytes=VMEM_LIMIT),
        name="inproj",
    )(x, g, wg, wm, ws, cw)


def _gla_kernel(z_ref, wlr_ref, blr_ref, gn_ref, o_ref, st_ref):
    tg = z_ref.shape[1]
    c = GLA_CHUNK

    @pl.when(pl.program_id(0) == 0)
    def _():
        st_ref[...] = jnp.zeros_like(st_ref)

    row = lax.broadcasted_iota(jnp.int32, (c, GLA_DK), 0)
    col_j = lax.broadcasted_iota(jnp.int32, (c, GLA_HEADS * c), 1) % c
    prow = lax.broadcasted_iota(jnp.int32, (c, GLA_HEADS * c), 0)
    arow = lax.broadcasted_iota(jnp.int32, (c, c), 0)
    acol = lax.broadcasted_iota(jnp.int32, (c, c), 1)
    tril = (arow >= acol).astype(BF16)
    ed = lax.broadcasted_iota(jnp.int32, (GLA_DK, GLA_HEADS * c), 0) // GLA_HK
    ej = lax.broadcasted_iota(jnp.int32, (GLA_DK, GLA_HEADS * c), 1) // c
    expand = (ed == ej).astype(BF16)
    kh = lax.broadcasted_iota(jnp.int32, (GLA_HEADS * c, GLA_DK), 0) // c
    kd_head = lax.broadcasted_iota(jnp.int32, (GLA_HEADS * c, GLA_DK), 1) // GLA_HK
    key_mask = kh == kd_head
    vh = lax.broadcasted_iota(jnp.int32, (GLA_HEADS * c, GLA_DV), 0) // c
    ve_head = lax.broadcasted_iota(jnp.int32, (GLA_HEADS * c, GLA_DV), 1) // GLA_HV
    val_mask = vh == ve_head
    se = lax.broadcasted_iota(jnp.int32, (GLA_DV, GLA_DK), 0) // GLA_HV
    sd = lax.broadcasted_iota(jnp.int32, (GLA_DV, GLA_DK), 1) // GLA_HK
    same_head = se == sd
    neg_inf = jnp.float32(-jnp.inf)
    w1, w2, w3 = _split3(wlr_ref[...])

    def one_batch(bi, rows):
        q = z_ref[bi, rows, 0:GLA_DK] * (GLA_HK ** -0.5)
        k = z_ref[bi, rows, GLA_DK:2 * GLA_DK]
        v = z_ref[bi, rows, 2 * GLA_DK:2 * GLA_DK + GLA_DV]
        r = z_ref[bi, rows, 2 * GLA_DK + GLA_DV:2 * GLA_DK + 2 * GLA_DV]
        lr = z_ref[bi, rows, 2 * GLA_DK + 2 * GLA_DV:GLA_IN]

        l1, l2, l3 = _split3(lr)
        zl = (_dot(l1, w1) + (_dot(l1, w2) + _dot(l2, w1))
              + (_dot(l1, w3) + _dot(l2, w2) + _dot(l3, w1))) + blr_ref[...]
        yield
        log_a = (jnp.minimum(zl, 0.0) - jnp.log(1.0 + jnp.exp(-jnp.abs(zl)))) * (1.0 / GLA_TAU)
        a1, a2, a3 = _split3(log_a)
        bc = _dot(tril, a1) + _dot(tril, a2) + _dot(tril, a3)
        yield

        att = jnp.zeros((c, GLA_HEADS * c), F32)
        s = c // 2
        while s >= GLA_SUB:
            ngrp = c // (2 * s)
            ref = jnp.concatenate(
                [jnp.broadcast_to(bc[g * 2 * s + s:g * 2 * s + s + 1, :], (2 * s, GLA_DK))
                 for g in range(ngrp)], axis=0)
            later = (row % (2 * s)) >= s
            ql = (q * jnp.exp(jnp.where(later, bc - ref, neg_inf))).astype(BF16)
            kl = (k * jnp.exp(jnp.where(later, neg_inf, ref - bc))).astype(BF16)
            kl_heads = jnp.where(key_mask, jnp.concatenate([kl] * GLA_HEADS, axis=0),
                                 jnp.zeros((), BF16))
            part = _dot_nt(ql, kl_heads)
            yield
            if ngrp > 1:
                part = jnp.where((prow // (2 * s)) == (col_j // (2 * s)), part, 0.0)
            att = att + part
            s //= 2

        prods = []
        for dlt in range(GLA_SUB):
            if dlt == 0:
                prods.append((q * k).astype(BF16))
            else:
                ok = (row % GLA_SUB) >= dlt
                kd = pltpu.roll(k, dlt, 0)
                bd = pltpu.roll(bc, dlt, 0)
                prods.append((q * kd * jnp.exp(jnp.where(ok, bc - bd, neg_inf))).astype(BF16))
        coef = _dot(jnp.concatenate(prods, axis=0), expand)
        yield
        for dlt in range(GLA_SUB):
            att = att + jnp.where(col_j == prow - dlt, coef[dlt * c:(dlt + 1) * c, :], 0.0)

        vb = v.astype(BF16)
        v_heads = jnp.where(val_mask, jnp.concatenate([vb] * GLA_HEADS, axis=0),
                            jnp.zeros((), BF16))
        o = _dot(att.astype(BF16), v_heads)
        yield
        st = st_ref[bi]
        o_state = _dot_nt((q * jnp.exp(bc)).astype(BF16), st.astype(BF16))
        yield
        o = o + o_state

        blast = bc[c - 1:c, :]
        kdec = (k * jnp.exp(blast - bc)).astype(BF16)
        upd = _dot_tn(vb, kdec)
        yield
        st_ref[bi] = st * jnp.exp(blast) + jnp.where(same_head, upd, 0.0)

        outs = []
        for hh in range(GLA_HEADS):
            oh = o[:, hh * GLA_HV:(hh + 1) * GLA_HV]
            outs.append(oh * lax.rsqrt(jnp.mean(oh * oh, axis=-1, keepdims=True) + EPS))
        on = jnp.concatenate(outs, axis=1) * gn_ref[...]
        o_ref[bi, rows, :] = (on * (r * jax.nn.sigmoid(r))).astype(o_ref.dtype)

    def chunk(ci, carry):
        rows = pl.ds(pl.multiple_of(ci * c, c), c)
        live = [one_batch(bi, rows) for bi in range(z_ref.shape[0])]
        while live:
            live = [gen for gen in live if next(gen, "done") != "done"]
        return carry

    lax.fori_loop(0, tg // c, chunk, 0)


def _gla(z, wlr, blr, gn, tg):
    bsz, t_len, _ = z.shape
    const = lambda t: (0, 0)
    return pl.pallas_call(
        _gla_kernel,
        grid=(t_len // tg,),
        in_specs=[
            pl.BlockSpec((bsz, tg, GLA_IN), lambda t: (0, t, 0)),
            pl.BlockSpec(wlr.shape, const),
            pl.BlockSpec(blr.shape, const),
            pl.BlockSpec(gn.shape, const),
        ],
        out_specs=pl.BlockSpec((bsz, tg, GLA_DV), lambda t: (0, t, 0)),
        out_shape=jax.ShapeDtypeStruct((bsz, t_len, GLA_DV), BF16),
        scratch_shapes=[pltpu.VMEM((bsz, GLA_DV, GLA_DK), F32)],
        compiler_params=pltpu.CompilerParams(
            dimension_semantics=("arbitrary",),
            vmem_limit_bytes=VMEM_LIMIT),
        name="gla",
    )(z, wlr, blr, gn)


def _kmean_kernel(k_ref, o_ref):
    nblk = k_ref.shape[1] // MOBA_BLOCK
    k = k_ref[0].reshape(nblk, MOBA_BLOCK, MOBA_W)
    o_ref[0] = jnp.sum(k, axis=1) * (1.0 / MOBA_BLOCK)


def _kmean(moba, rows):
    bsz, t_len, _ = moba.shape
    return pl.pallas_call(
        _kmean_kernel,
        grid=(bsz, t_len // rows),
        in_specs=[pl.BlockSpec((1, rows, MOBA_W), lambda b, t: (b, t, 1))],
        out_specs=pl.BlockSpec((1, rows // MOBA_BLOCK, MOBA_W), lambda b, t: (b, t, 0)),
        out_shape=jax.ShapeDtypeStruct((bsz, t_len // MOBA_BLOCK, MOBA_W), F32),
        compiler_params=pltpu.CompilerParams(
            dimension_semantics=("arbitrary", "arbitrary"),
            vmem_limit_bytes=VMEM_LIMIT),
        name="moba_kmean",
    )(moba)


def _moba_prep_kernel(m_ref, km_ref, qa_ref, ka_ref, vt_ref):
    nb = MOBA_MAXB
    n_own = pl.program_id(1)
    n_own_f = n_own.astype(F32)
    q = m_ref[0, :, 0:MOBA_W]
    k = m_ref[0, :, MOBA_W:2 * MOBA_W]
    v = m_ref[0, :, 2 * MOBA_W:3 * MOBA_W]
    qt = q.T * (MOBA_HD ** -0.5)
    vt_ref[0, :, 0, 0:MOBA_HD, :] = v.T.reshape(MOBA_HEADS, MOBA_HD, MOBA_BLOCK).astype(BF16)
    tail = lax.broadcasted_iota(jnp.int32, (MOBA_HEADS, MOBA_VROWS - MOBA_HD, MOBA_BLOCK), 1)
    vt_ref[0, :, 0, MOBA_HD:MOBA_VROWS, :] = (tail == 0).astype(BF16)

    brow = lax.broadcasted_iota(jnp.int32, (nb, MOBA_BLOCK), 0).astype(F32)
    neg_inf = jnp.float32(-jnp.inf)
    lane = lax.broadcasted_iota(jnp.int32, (MOBA_BLOCK, MOBA_AUG), 1)
    pos = lax.broadcasted_iota(jnp.int32, (MOBA_BLOCK, MOBA_AUG), 0).astype(F32)

    for hh in range(MOBA_HEADS):
        slope = 2.0 ** (-8.0 * (hh + 1) / MOBA_HEADS)
        qth = qt[hh * MOBA_HD:(hh + 1) * MOBA_HD, :]
        q1, q2, q3 = _split3(qth)
        m1, m2, m3 = _split3(km_ref[0, hh])
        gate = (_dot(m1, q1) + (_dot(m1, q2) + _dot(m2, q1))
                + (_dot(m1, q3) + _dot(m2, q2) + _dot(m3, q1)))
        gate = jnp.where(brow < n_own_f, gate, neg_inf)
        chosen = jnp.zeros((nb, MOBA_BLOCK), jnp.bool_)
        for _ in range(MOBA_TOPK):
            best = jnp.max(gate, axis=0, keepdims=True)
            first = jnp.min(jnp.where(gate == best, brow, float(nb)), axis=0, keepdims=True)
            hit = brow == first
            chosen = chosen | (hit & (best > neg_inf))
            gate = jnp.where(hit, neg_inf, gate)
        term = jnp.where(chosen, (slope * MOBA_BLOCK) * (brow - n_own_f), MOBA_NEG)
        term = jnp.where(brow == n_own_f, 0.0, term)
        term = jnp.where(brow == float(nb - 1), 1.0, term)
        halves = [qth, term] if hh % 2 == 0 else [term, qth]
        qa_ref[0, hh] = jnp.concatenate(halves, axis=0).astype(BF16)

        kpair = k[:, (hh // 2) * MOBA_AUG:(hh // 2 + 1) * MOBA_AUG]
        idx = lane - MOBA_HD if hh % 2 == 0 else lane
        extra = jnp.where(idx == nb - 1, slope * pos, (idx == n_own).astype(F32))
        keep = (lane < MOBA_HD) if hh % 2 == 0 else (lane >= MOBA_HD)
        ka_ref[0, hh] = jnp.where(keep, kpair, extra).astype(BF16)


def _moba_prep(moba, km):
    bsz, t_len, _ = moba.shape
    nb = t_len // MOBA_BLOCK
    return pl.pallas_call(
        _moba_prep_kernel,
        grid=(bsz, nb),
        in_specs=[
            pl.BlockSpec((1, MOBA_BLOCK, 3 * MOBA_W), lambda b, t: (b, t, 0)),
            pl.BlockSpec((1, MOBA_HEADS, MOBA_MAXB, MOBA_HD), lambda b, t: (b, 0, 0, 0)),
        ],
        out_specs=[
            pl.BlockSpec((1, MOBA_HEADS, MOBA_AUG, MOBA_BLOCK), lambda b, t: (b, 0, 0, t)),
            pl.BlockSpec((1, MOBA_HEADS, MOBA_BLOCK, MOBA_AUG), lambda b, t: (b, 0, t, 0)),
            pl.BlockSpec((1, MOBA_HEADS, 1, MOBA_VROWS, MOBA_BLOCK), lambda b, t: (b, 0, t, 0, 0)),
        ],
        out_shape=[
            jax.ShapeDtypeStruct((bsz, MOBA_HEADS, MOBA_AUG, t_len), BF16),
            jax.ShapeDtypeStruct((bsz, MOBA_HEADS, t_len, MOBA_AUG), BF16),
            jax.ShapeDtypeStruct((bsz, MOBA_HEADS, nb, MOBA_VROWS, MOBA_BLOCK), BF16),
        ],
        compiler_params=pltpu.CompilerParams(
            dimension_semantics=("arbitrary", "arbitrary"),
            vmem_limit_bytes=VMEM_LIMIT),
        name="moba_prep",
    )(moba, km)


def _moba_attn_kernel(qa_ref, ka_ref, vt_ref, o_ref, s_scr, p_scr, acc_scr):
    j = pl.program_id(2)
    qt = qa_ref.shape[3]
    ngrp = qt // MOBA_QG
    kb = MOBA_KT // MOBA_BLOCK
    dsteps = qt // MOBA_KT
    base = j * dsteps

    def keys(step):
        r0 = pl.multiple_of(step * MOBA_KT, MOBA_KT)
        return ka_ref[0, 0, pl.ds(r0, MOBA_KT), :]

    def score_stage(buf, step, g, mask_from=None):
        s = _dot(keys(step), qa_ref[0, 0, :, g * MOBA_QG:(g + 1) * MOBA_QG])
        if mask_from is not None:
            krow = lax.broadcasted_iota(jnp.int32, s.shape, 0) + mask_from
            qcol = lax.broadcasted_iota(jnp.int32, s.shape, 1) + g * MOBA_QG
            s = jnp.where(krow <= qcol, s, -jnp.inf)
        s_scr[buf] = s

    def softmax_stage(buf, pbuf, m):
        s = s_scr[buf]
        m_new = jnp.max(s, axis=0, keepdims=True)
        alpha = None
        if m is not None:
            m_new = jnp.maximum(m, m_new)
            alpha = jnp.exp(m - m_new)
        p_scr[pbuf] = jnp.exp(s - m_new).astype(BF16)
        return m_new, alpha

    def value_stage(buf, g, alpha, step):
        pv = _dot(vt_ref[0, 0, step * kb], p_scr[buf, 0:MOBA_BLOCK, :])
        for i in range(1, kb):
            pv = pv + _dot(vt_ref[0, 0, step * kb + i],
                           p_scr[buf, i * MOBA_BLOCK:(i + 1) * MOBA_BLOCK, :])
        acc_scr[g] = pv if alpha is None else alpha * acc_scr[g] + pv

    items = []
    for d in range(dsteps):
        k_lo, k_hi = d * MOBA_KT, (d + 1) * MOBA_KT - 1
        for g in range(ngrp):
            q_lo, q_hi = g * MOBA_QG, (g + 1) * MOBA_QG - 1
            if k_lo <= q_hi:
                items.append((d, g, k_lo if k_hi > q_lo else None))
    nbuf, ahead = s_scr.shape[0], MOBA_AHEAD
    assert ngrp % nbuf == 0 and ngrp % 2 == 0 and len(items) % 2 == 0 and ahead < nbuf
    assert items[-1][1] == ngrp - 1 and ahead <= ngrp
    par0 = len(items) % nbuf

    def diag_score(i):
        if i < len(items):
            score_stage(i % nbuf, base + items[i][0], items[i][1], items[i][2])
        else:
            score_stage(i % nbuf, 0, i - len(items))

    ms = [None] * ngrp
    for i in range(ahead):
        diag_score(i)
    pend = None
    for i, (d, g, _) in enumerate(items):
        diag_score(i + ahead)
        ms[g], alpha = softmax_stage(i % nbuf, i % 2, ms[g])
        if pend is not None:
            value_stage((i - 1) % 2, *pend)
        pend = (g, alpha, base + d)

    def body(n, carry):
        ms, alpha_p, step_p = carry
        ms = list(ms)
        pend = (ngrp - 1, alpha_p, step_p)
        for g in range(ngrp):
            if g + ahead < ngrp:
                score_stage((par0 + g + ahead) % nbuf, n, g + ahead)
            else:
                score_stage((par0 + g + ahead) % nbuf, jnp.minimum(n + 1, base - 1),
                            g + ahead - ngrp)
            ms[g], alpha = softmax_stage((par0 + g) % nbuf, g % 2, ms[g])
            value_stage((g - 1) % 2, *pend)
            pend = (g, alpha, n)
        return tuple(ms), pend[1], pend[2]

    _, alpha_p, step_p = lax.fori_loop(0, base, body, (tuple(ms), pend[1], pend[2]))
    value_stage(1, ngrp - 1, alpha_p, step_p)
    for g in range(ngrp):
        num, den = acc_scr[g, 0:MOBA_HD, :], acc_scr[g, MOBA_HD:MOBA_HD + 1, :]
        o_ref[0, :, g * MOBA_QG:(g + 1) * MOBA_QG] = (num / den).astype(o_ref.dtype)


def _moba_attn(qa, ka, vt):
    bsz, _, _, t_len = qa.shape
    nb = t_len // MOBA_BLOCK
    qt = min(MOBA_QT, t_len)
    return pl.pallas_call(
        _moba_attn_kernel,
        grid=(bsz, MOBA_HEADS, t_len // qt),
        in_specs=[
            pl.BlockSpec((1, 1, MOBA_AUG, qt), lambda b, h, t: (b, h, 0, t)),
            pl.BlockSpec((1, 1, t_len, MOBA_AUG), lambda b, h, t: (b, h, 0, 0)),
            pl.BlockSpec((1, 1, nb, MOBA_VROWS, MOBA_BLOCK), lambda b, h, t: (b, h, 0, 0, 0)),
        ],
        out_specs=pl.BlockSpec((1, MOBA_HD, qt), lambda b, h, t: (b, h, t)),
        out_shape=jax.ShapeDtypeStruct((bsz, MOBA_W, t_len), BF16),
        scratch_shapes=[
            pltpu.VMEM((MOBA_SBUF, MOBA_KT, MOBA_QG), F32),
            pltpu.VMEM((2, MOBA_KT, MOBA_QG), BF16),
            pltpu.VMEM((qt // MOBA_QG, MOBA_VROWS, MOBA_QG), F32),
        ],
        compiler_params=pltpu.CompilerParams(
            dimension_semantics=("arbitrary", "arbitrary", "arbitrary"),
            vmem_limit_bytes=VMEM_LIMIT),
        name="moba_attn",
    )(qa, ka, vt)


def _merge_kernel(x_ref, gpre_ref, bra_ref, brbt_ref, brc_ref, wgate_ref,
                  wa_ref, wb_ref, wc_ref, wout_ref, gpost_ref, o_ref):
    x = x_ref[0]
    h = _rms(x, gpre_ref[...]).astype(BF16)
    merged = jax.nn.sigmoid(_dot(h, wgate_ref[:, 0:D_MODEL])) * _dot(bra_ref[0], wa_ref[...])
    merged = merged + (jax.nn.sigmoid(_dot(h, wgate_ref[:, D_MODEL:2 * D_MODEL]))
                       * _dot_tn(brbt_ref[0], wb_ref[...]))
    merged = merged + (jax.nn.sigmoid(_dot(h, wgate_ref[:, 2 * D_MODEL:3 * D_MODEL]))
                       * _dot(brc_ref[0], wc_ref[...]))
    y = _dot(merged.astype(BF16), wout_ref[...])
    o_ref[0] = x + _rms(y, gpost_ref[...])


def _merge(x, gpre, bra, brbt, brc, wgate, wa, wb, wc, wout, gpost, tm):
    bsz, t_len, _ = x.shape
    const = lambda b, t: (0, 0)
    return pl.pallas_call(
        _merge_kernel,
        grid=(bsz, t_len // tm),
        in_specs=[
            pl.BlockSpec((1, tm, D_MODEL), lambda b, t: (b, t, 0)),
            pl.BlockSpec((1, D_MODEL), const),
            pl.BlockSpec((1, tm, GLA_DV), lambda b, t: (b, t, 0)),
            pl.BlockSpec((1, MOBA_W, tm), lambda b, t: (b, 0, t)),
            pl.BlockSpec((1, tm, SC_W), lambda b, t: (b, t, 0)),
            pl.BlockSpec(wgate.shape, const),
            pl.BlockSpec(wa.shape, const),
            pl.BlockSpec(wb.shape, const),
            pl.BlockSpec(wc.shape, const),
            pl.BlockSpec(wout.shape, const),
            pl.BlockSpec((1, D_MODEL), const),
        ],
        out_specs=pl.BlockSpec((1, tm, D_MODEL), lambda b, t: (b, t, 0)),
        out_shape=jax.ShapeDtypeStruct(x.shape, x.dtype),
        compiler_params=pltpu.CompilerParams(
            dimension_semantics=("arbitrary", "arbitrary"),
            vmem_limit_bytes=VMEM_LIMIT),
        name="merge",
    )(x, gpre, bra, brbt, brc, wgate, wa, wb, wc, wout, gpost)


def _ffn_kernel(x_ref, gpre_ref, wup_ref, cw_ref, cb_ref, wdn_ref, gpost_ref,
                o_ref, halo_ref):
    tm = x_ref.shape[1]

    @pl.when(pl.program_id(1) == 0)
    def _():
        halo_ref[...] = jnp.zeros_like(halo_ref)

    x = x_ref[0]
    h = _rms(x, gpre_ref[...]).astype(BF16)
    y = jnp.zeros((tm, D_MODEL), F32)
    for ci in range(D_FF // FFN_CHUNK):
        halves = []
        for base in (ci * FFN_CHUNK, D_FF + ci * FFN_CHUNK):
            cols = slice(base, base + FFN_CHUNK)
            u = _dot(h, wup_ref[:, cols])
            conv = _causal_conv3(u, halo_ref[:, cols], cw_ref[:, cols]) + cb_ref[:, cols]
            halo_ref[:, cols] = u[tm - 8:tm, :]
            halves.append(conv)
        act = (halves[0] * jax.nn.sigmoid(halves[0]) * halves[1]).astype(BF16)
        y = y + _dot(act, wdn_ref[ci * FFN_CHUNK:(ci + 1) * FFN_CHUNK, :])
    o_ref[0] = x + _rms(y, gpost_ref[...])


def _ffn(x, gpre, wup, cw, cb, wdn, gpost, tm):
    bsz, t_len, _ = x.shape
    const = lambda b, t: (0, 0)
    return pl.pallas_call(
        _ffn_kernel,
        grid=(bsz, t_len // tm),
        in_specs=[
            pl.BlockSpec((1, tm, D_MODEL), lambda b, t: (b, t, 0)),
            pl.BlockSpec((1, D_MODEL), const),
            pl.BlockSpec(wup.shape, const, pipeline_mode=pl.Buffered(1)),
            pl.BlockSpec(cw.shape, const),
            pl.BlockSpec(cb.shape, const),
            pl.BlockSpec(wdn.shape, const, pipeline_mode=pl.Buffered(1)),
            pl.BlockSpec((1, D_MODEL), const),
        ],
        out_specs=pl.BlockSpec((1, tm, D_MODEL), lambda b, t: (b, t, 0)),
        out_shape=jax.ShapeDtypeStruct(x.shape, x.dtype),
        scratch_shapes=[pltpu.VMEM((8, 2 * D_FF), F32)],
        compiler_params=pltpu.CompilerParams(
            dimension_semantics=("arbitrary", "arbitrary"),
            vmem_limit_bytes=VMEM_LIMIT),
        name="ffn",
    )(x, gpre, wup, cw, cb, wdn, gpost)


def _pad_rows(w, rows):
    return jnp.concatenate([w, jnp.zeros((rows - w.shape[0],) + w.shape[1:], w.dtype)], axis=0)


def kernel(x, g_mix_pre, w_in, gla_w_lr2, gla_b_lr, gla_norm, sc_conv_w, w_br_gla, w_br_moba,
           w_br_sc, w_out, g_mix_post, g_ffn_pre, ffn_w_up, ffn_conv_w, ffn_conv_b, ffn_w_down,
           g_ffn_post):
    bsz, t_len, _ = x.shape
    depth = w_in.shape[0]
    tm = min(512, t_len)
    tg = min(256, t_len)
    nb = t_len // MOBA_BLOCK
    gla_cols = 2 * GLA_DK + 2 * GLA_DV + GLA_LOWRANK
    moba_end = gla_cols + 3 * MOBA_W
    sc_end = moba_end + 3 * SC_W

    for l in range(depth):
        w = w_in[l]
        wg = jnp.concatenate(
            [w[:, :gla_cols], jnp.zeros((D_MODEL, GLA_LR_PAD - GLA_LOWRANK), w.dtype)],
            axis=1).astype(BF16)
        wm = w[:, gla_cols:moba_end].astype(BF16)
        ws = w[:, moba_end:sc_end].astype(BF16)
        wgate = w[:, sc_end:].astype(BF16)
        row = lambda a: a.reshape(1, -1)

        gla_in, moba, br_c = _inproj(x, row(g_mix_pre[l]), wg, wm, ws,
                                     _pad_rows(sc_conv_w[l], 8), tm)
        br_a = _gla(gla_in, _pad_rows(gla_w_lr2[l], GLA_LR_PAD), row(gla_b_lr[l]),
                    row(jnp.tile(gla_norm[l], GLA_HEADS)), tg)
        km = _kmean(moba, min(2048, t_len))
        km = km.reshape(bsz, nb, MOBA_HEADS, MOBA_HD).transpose(0, 2, 1, 3)
        km = jnp.pad(km, ((0, 0), (0, 0), (0, MOBA_MAXB - nb), (0, 0)))
        qa, ka, vt = _moba_prep(moba, km)
        br_bt = _moba_attn(qa, ka, vt)
        x = _merge(x, row(g_mix_pre[l]), br_a, br_bt, br_c, wgate,
                   w_br_gla[l].astype(BF16), w_br_moba[l].astype(BF16),
                   w_br_sc[l].astype(BF16), w_out[l].astype(BF16), row(g_mix_post[l]), tm)
        x = _ffn(x, row(g_ffn_pre[l]), ffn_w_up[l].astype(BF16), _pad_rows(ffn_conv_w[l], 8),
                 row(ffn_conv_b[l]), ffn_w_down[l].astype(BF16), row(g_ffn_post[l]), tm)
    return x
```

```python
import functools

import jax
import jax.numpy as jnp
from jax import lax
from jax.experimental import pallas as pl
from jax.experimental.pallas import tpu as pltpu

F32 = jnp.float32
BF16 = jnp.bfloat16

D_MODEL = 1024
EPS = 1e-6

GLA_HEADS = 4
GLA_DK = 256
GLA_DV = 512
GLA_HK = GLA_DK // GLA_HEADS
GLA_HV = GLA_DV // GLA_HEADS
GLA_LOWRANK = 16
GLA_TAU = 16.0
GLA_CHUNK = 64
GLA_SUB = 8
GLA_LR_PAD = 128
GLA_IN = 2 * GLA_DK + 2 * GLA_DV + GLA_LR_PAD

MOBA_HEADS = 8
MOBA_HD = 64
MOBA_W = MOBA_HEADS * MOBA_HD
MOBA_BLOCK = 256
MOBA_TOPK = 3
MOBA_AUG = 2 * MOBA_HD
MOBA_MAXB = MOBA_AUG - MOBA_HD
MOBA_NEG = -65536.0
MOBA_QT = 4096
MOBA_KT = 512
MOBA_QG = 256
MOBA_SBUF = 4
MOBA_AHEAD = 3
MOBA_VROWS = MOBA_HD + 16

SC_W = 512
D_FF = 2816
FFN_CHUNK = 1408

VMEM_LIMIT = 56 * 1024 * 1024


def _rms(x, g):
    return x * lax.rsqrt(jnp.mean(x * x, axis=-1, keepdims=True) + EPS) * g


def _dot(a, b):
    return jnp.dot(a, b, preferred_element_type=F32)


def _dot_nt(a, b):
    return lax.dot_general(a, b, (((1,), (1,)), ((), ())), preferred_element_type=F32)


def _dot_tn(a, b):
    return lax.dot_general(a, b, (((0,), (0,)), ((), ())), preferred_element_type=F32)


def _split3(a):
    hi = a.astype(BF16)
    r1 = a - hi.astype(F32)
    mid = r1.astype(BF16)
    lo = (r1 - mid.astype(F32)).astype(BF16)
    return hi, mid, lo


def _causal_conv3(p, halo, w):
    row = lax.broadcasted_iota(jnp.int32, p.shape, 0)
    p1 = pltpu.roll(p, 1, 0)
    p1 = jnp.where(row == 0, halo[7:8, :], p1)
    p2 = pltpu.roll(p, 2, 0)
    p2 = jnp.where(row == 0, halo[6:7, :], jnp.where(row == 1, halo[7:8, :], p2))
    return w[0:1, :] * p2 + w[1:2, :] * p1 + w[2:3, :] * p


def _inproj_kernel(x_ref, g_ref, wg_ref, wm_ref, ws_ref, cw_ref,
                   gla_ref, moba_ref, brc_ref, km_ref, halo_ref):
    tm = x_ref.shape[1]

    @pl.when(pl.program_id(1) == 0)
    def _():
        halo_ref[...] = jnp.zeros_like(halo_ref)

    h = _rms(x_ref[0], g_ref[...]).astype(BF16)
    gla_ref[0] = _dot(h, wg_ref[...])
    mo = _dot(h, wm_ref[...])
    moba_ref[0] = mo
    kblk = mo[:, MOBA_W:2 * MOBA_W].reshape(tm // MOBA_BLOCK, MOBA_BLOCK, MOBA_W)
    km_ref[0, 0] = jnp.sum(kblk, axis=1) * (1.0 / MOBA_BLOCK)
    s = _dot(h, ws_ref[...])
    sb = s[:, 0:SC_W]
    p = s[:, SC_W:2 * SC_W] * s[:, 2 * SC_W:3 * SC_W]
    y = _causal_conv3(p, halo_ref[...], cw_ref[...])
    halo_ref[...] = p[tm - 8:tm, :]
    brc_ref[0] = (sb * y).astype(BF16)


def _inproj(x, g, wg, wm, ws, cw, tm):
    bsz, t_len, _ = x.shape
    const = lambda b, t: (0, 0)
    return pl.pallas_call(
        _inproj_kernel,
        grid=(bsz, t_len // tm),
        in_specs=[
            pl.BlockSpec((1, tm, D_MODEL), lambda b, t: (b, t, 0)),
            pl.BlockSpec((1, D_MODEL), const),
            pl.BlockSpec(wg.shape, const),
            pl.BlockSpec(wm.shape, const),
            pl.BlockSpec(ws.shape, const),
            pl.BlockSpec(cw.shape, const),
        ],
        out_specs=[
            pl.BlockSpec((1, tm, GLA_IN), lambda b, t: (b, t, 0)),
            pl.BlockSpec((1, tm, 3 * MOBA_W), lambda b, t: (b, t, 0)),
            pl.BlockSpec((1, tm, SC_W), lambda b, t: (b, t, 0)),
            pl.BlockSpec((1, 1, tm // MOBA_BLOCK, MOBA_W), lambda b, t: (b, t, 0, 0)),
        ],
        out_shape=[
            jax.ShapeDtypeStruct((bsz, t_len, GLA_IN), F32),
            jax.ShapeDtypeStruct((bsz, t_len, 3 * MOBA_W), F32),
            jax.ShapeDtypeStruct((bsz, t_len, SC_W), BF16),
            jax.ShapeDtypeStruct((bsz, t_len // tm, tm // MOBA_BLOCK, MOBA_W), F32),
        ],
        scratch_shapes=[pltpu.VMEM((8, SC_W), F32)],
        compiler_params=pltpu.CompilerParams(
            dimension_semantics=("arbitrary", "arbitrary"),
            vmem_limit_bytes=VMEM_LIMIT),
        name="inproj",
    )(x, g, wg, wm, ws, cw)


def _gla_kernel(z_ref, wlr_ref, blr_ref, gn_ref, o_ref, st_ref):
    tg = z_ref.shape[1]
    c = GLA_CHUNK

    @pl.when(pl.program_id(0) == 0)
    def _():
        st_ref[...] = jnp.zeros_like(st_ref)

    row = lax.broadcasted_iota(jnp.int32, (c, GLA_DK), 0)
    col_j = lax.broadcasted_iota(jnp.int32, (c, GLA_HEADS * c), 1) % c
    prow = lax.broadcasted_iota(jnp.int32, (c, GLA_HEADS * c), 0)
    arow = lax.broadcasted_iota(jnp.int32, (c, c), 0)
    acol = lax.broadcasted_iota(jnp.int32, (c, c), 1)
    tril = (arow >= acol).astype(BF16)
    ed = lax.broadcasted_iota(jnp.int32, (GLA_DK, GLA_HEADS * c), 0) // GLA_HK
    ej = lax.broadcasted_iota(jnp.int32, (GLA_DK, GLA_HEADS * c), 1) // c
    expand = (ed == ej).astype(BF16)
    kh = lax.broadcasted_iota(jnp.int32, (GLA_HEADS * c, GLA_DK), 0) // c
    kd_head = lax.broadcasted_iota(jnp.int32, (GLA_HEADS * c, GLA_DK), 1) // GLA_HK
    key_mask = kh == kd_head
    vh = lax.broadcasted_iota(jnp.int32, (GLA_HEADS * c, GLA_DV), 0) // c
    ve_head = lax.broadcasted_iota(jnp.int32, (GLA_HEADS * c, GLA_DV), 1) // GLA_HV
    val_mask = vh == ve_head
    se = lax.broadcasted_iota(jnp.int32, (GLA_DV, GLA_DK), 0) // GLA_HV
    sd = lax.broadcasted_iota(jnp.int32, (GLA_DV, GLA_DK), 1) // GLA_HK
    same_head = se == sd
    neg_inf = jnp.float32(-jnp.inf)
    w1, w2, w3 = _split3(wlr_ref[...])

    def one_batch(bi, rows):
        q = z_ref[bi, rows, 0:GLA_DK] * (GLA_HK ** -0.5)
        k = z_ref[bi, rows, GLA_DK:2 * GLA_DK]
        v = z_ref[bi, rows, 2 * GLA_DK:2 * GLA_DK + GLA_DV]
        r = z_ref[bi, rows, 2 * GLA_DK + GLA_DV:2 * GLA_DK + 2 * GLA_DV]
        lr = z_ref[bi, rows, 2 * GLA_DK + 2 * GLA_DV:GLA_IN]

        l1, l2, l3 = _split3(lr)
        zl = (_dot(l1, w1) + (_dot(l1, w2) + _dot(l2, w1))
              + (_dot(l1, w3) + _dot(l2, w2) + _dot(l3, w1))) + blr_ref[...]
        yield
        log_a = (jnp.minimum(zl, 0.0) - jnp.log(1.0 + jnp.exp(-jnp.abs(zl)))) * (1.0 / GLA_TAU)
        a1, a2, a3 = _split3(log_a)
        bc = _dot(tril, a1) + _dot(tril, a2) + _dot(tril, a3)
        yield

        att = jnp.zeros((c, GLA_HEADS * c), F32)
        s = c // 2
        while s >= GLA_SUB:
            ngrp = c // (2 * s)
            ref = jnp.concatenate(
                [jnp.broadcast_to(bc[g * 2 * s + s:g * 2 * s + s + 1, :], (2 * s, GLA_DK))
                 for g in range(ngrp)], axis=0)
            later = (row % (2 * s)) >= s
            ql = (q * jnp.exp(jnp.where(later, bc - ref, neg_inf))).astype(BF16)
            kl = (k * jnp.exp(jnp.where(later, neg_inf, ref - bc))).astype(BF16)
            kl_heads = jnp.where(key_mask, jnp.concatenate([kl] * GLA_HEADS, axis=0),
                                 jnp.zeros((), BF16))
            part = _dot_nt(ql, kl_heads)
            yield
            if ngrp > 1:
                part = jnp.where((prow // (2 * s)) == (col_j // (2 * s)), part, 0.0)
            att = att + part
            s //= 2

        prods = []
        for dlt in range(GLA_SUB):
            if dlt == 0:
                prods.append((q * k).astype(BF16))
            else:
                ok = (row % GLA_SUB) >= dlt
                kd = pltpu.roll(k, dlt, 0)
                bd = pltpu.roll(bc, dlt, 0)
                prods.append((q * kd * jnp.exp(jnp.where(ok, bc - bd, neg_inf))).astype(BF16))
        coef = _dot(jnp.concatenate(prods, axis=0), expand)
        yield
        for dlt in range(GLA_SUB):
            att = att + jnp.where(col_j == prow - dlt, coef[dlt * c:(dlt + 1) * c, :], 0.0)

        vb = v.astype(BF16)
        v_heads = jnp.where(val_mask, jnp.concatenate([vb] * GLA_HEADS, axis=0),
                            jnp.zeros((), BF16))
        o = _dot(att.astype(BF16), v_heads)
        yield
        st = st_ref[bi]
        o_state = _dot_nt((q * jnp.exp(bc)).astype(BF16), st.astype(BF16))
        yield
        o = o + o_state

        blast = bc[c - 1:c, :]
        kdec = (k * jnp.exp(blast - bc)).astype(BF16)
        upd = _dot_tn(vb, kdec)
        yield
        st_ref[bi] = st * jnp.exp(blast) + jnp.where(same_head, upd, 0.0)

        outs = []
        for hh in range(GLA_HEADS):
            oh = o[:, hh * GLA_HV:(hh + 1) * GLA_HV]
            outs.append(oh * lax.rsqrt(jnp.mean(oh * oh, axis=-1, keepdims=True) + EPS))
        on = jnp.concatenate(outs, axis=1) * gn_ref[...]
        o_ref[bi, rows, :] = (on * (r * jax.nn.sigmoid(r))).astype(o_ref.dtype)

    def chunk(ci, carry):
        rows = pl.ds(pl.multiple_of(ci * c, c), c)
        live = [one_batch(bi, rows) for bi in range(z_ref.shape[0])]
        while live:
            live = [gen for gen in live if next(gen, "done") != "done"]
        return carry

    lax.fori_loop(0, tg // c, chunk, 0)


def _gla(z, wlr, blr, gn, tg):
    bsz, t_len, _ = z.shape
    const = lambda t: (0, 0)
    return pl.pallas_call(
        _gla_kernel,
        grid=(t_len // tg,),
        in_specs=[
            pl.BlockSpec((bsz, tg, GLA_IN), lambda t: (0, t, 0)),
            pl.BlockSpec(wlr.shape, const),
            pl.BlockSpec(blr.shape, const),
            pl.BlockSpec(gn.shape, const),
        ],
        out_specs=pl.BlockSpec((bsz, tg, GLA_DV), lambda t: (0, t, 0)),
        out_shape=jax.ShapeDtypeStruct((bsz, t_len, GLA_DV), BF16),
        scratch_shapes=[pltpu.VMEM((bsz, GLA_DV, GLA_DK), F32)],
        compiler_params=pltpu.CompilerParams(
            dimension_semantics=("arbitrary",),
            vmem_limit_bytes=VMEM_LIMIT),
        name="gla",
    )(z, wlr, blr, gn)


def _kmean_kernel(k_ref, o_ref):
    nblk = k_ref.shape[1] // MOBA_BLOCK
    k = k_ref[0].reshape(nblk, MOBA_BLOCK, MOBA_W)
    o_ref[0] = jnp.sum(k, axis=1) * (1.0 / MOBA_BLOCK)


def _kmean(moba, rows):
    bsz, t_len, _ = moba.shape
    return pl.pallas_call(
        _kmean_kernel,
        grid=(bsz, t_len // rows),
        in_specs=[pl.BlockSpec((1, rows, MOBA_W), lambda b, t: (b, t, 1))],
        out_specs=pl.BlockSpec((1, rows // MOBA_BLOCK, MOBA_W), lambda b, t: (b, t, 0)),
        out_shape=jax.ShapeDtypeStruct((bsz, t_len // MOBA_BLOCK, MOBA_W), F32),
        compiler_params=pltpu.CompilerParams(
            dimension_semantics=("arbitrary", "arbitrary"),
            vmem_limit_bytes=VMEM_LIMIT),
        name="moba_kmean",
    )(moba)


def _moba_prep_kernel(m_ref, km_ref, qa_ref, ka_ref, vt_ref):
    nb = MOBA_MAXB
    n_own = pl.program_id(1)
    n_own_f = n_own.astype(F32)
    q = m_ref[0, :, 0:MOBA_W]
    k = m_ref[0, :, MOBA_W:2 * MOBA_W]
    v = m_ref[0, :, 2 * MOBA_W:3 * MOBA_W]
    qt = q.T * (MOBA_HD ** -0.5)
    vt_ref[0, :, 0, 0:MOBA_HD, :] = v.T.reshape(MOBA_HEADS, MOBA_HD, MOBA_BLOCK).astype(BF16)
    tail = lax.broadcasted_iota(jnp.int32, (MOBA_HEADS, MOBA_VROWS - MOBA_HD, MOBA_BLOCK), 1)
    vt_ref[0, :, 0, MOBA_HD:MOBA_VROWS, :] = (tail == 0).astype(BF16)

    brow = lax.broadcasted_iota(jnp.int32, (nb, MOBA_BLOCK), 0).astype(F32)
    neg_inf = jnp.float32(-jnp.inf)
    lane = lax.broadcasted_iota(jnp.int32, (MOBA_BLOCK, MOBA_AUG), 1)
    pos = lax.broadcasted_iota(jnp.int32, (MOBA_BLOCK, MOBA_AUG), 0).astype(F32)

    for hh in range(MOBA_HEADS):
        slope = 2.0 ** (-8.0 * (hh + 1) / MOBA_HEADS)
        qth = qt[hh * MOBA_HD:(hh + 1) * MOBA_HD, :]
        q1, q2, q3 = _split3(qth)
        m1, m2, m3 = _split3(km_ref[0, hh])
        gate = (_dot(m1, q1) + (_dot(m1, q2) + _dot(m2, q1))
                + (_dot(m1, q3) + _dot(m2, q2) + _dot(m3, q1)))
        gate = jnp.where(brow < n_own_f, gate, neg_inf)
        chosen = jnp.zeros((nb, MOBA_BLOCK), jnp.bool_)
        for _ in range(MOBA_TOPK):
            best = jnp.max(gate, axis=0, keepdims=True)
            first = jnp.min(jnp.where(gate == best, brow, float(nb)), axis=0, keepdims=True)
            hit = brow == first
            chosen = chosen | (hit & (best > neg_inf))
            gate = jnp.where(hit, neg_inf, gate)
        term = jnp.where(chosen, (slope * MOBA_BLOCK) * (brow - n_own_f), MOBA_NEG)
        term = jnp.where(brow == n_own_f, 0.0, term)
        term = jnp.where(brow == float(nb - 1), 1.0, term)
        halves = [qth, term] if hh % 2 == 0 else [term, qth]
        qa_ref[0, hh] = jnp.concatenate(halves, axis=0).astype(BF16)

        kpair = k[:, (hh // 2) * MOBA_AUG:(hh // 2 + 1) * MOBA_AUG]
        idx = lane - MOBA_HD if hh % 2 == 0 else lane
        extra = jnp.where(idx == nb - 1, slope * pos, (idx == n_own).astype(F32))
        keep = (lane < MOBA_HD) if hh % 2 == 0 else (lane >= MOBA_HD)
        ka_ref[0, hh] = jnp.where(keep, kpair, extra).astype(BF16)


def _moba_prep(moba, km):
    bsz, t_len, _ = moba.shape
    nb = t_len // MOBA_BLOCK
    return pl.pallas_call(
        _moba_prep_kernel,
        grid=(bsz, nb),
        in_specs=[
            pl.BlockSpec((1, MOBA_BLOCK, 3 * MOBA_W), lambda b, t: (b, t, 0)),
            pl.BlockSpec((1, MOBA_HEADS, MOBA_MAXB, MOBA_HD), lambda b, t: (b, 0, 0, 0)),
        ],
        out_specs=[
            pl.BlockSpec((1, MOBA_HEADS, MOBA_AUG, MOBA_BLOCK), lambda b, t: (b, 0, 0, t)),
            pl.BlockSpec((1, MOBA_HEADS, MOBA_BLOCK, MOBA_AUG), lambda b, t: (b, 0, t, 0)),
            pl.BlockSpec((1, MOBA_HEADS, 1, MOBA_VROWS, MOBA_BLOCK), lambda b, t: (b, 0, t, 0, 0)),
        ],
        out_shape=[
            jax.ShapeDtypeStruct((bsz, MOBA_HEADS, MOBA_AUG, t_len), BF16),
            jax.ShapeDtypeStruct((bsz, MOBA_HEADS, t_len, MOBA_AUG), BF16),
            jax.ShapeDtypeStruct((bsz, MOBA_HEADS, nb, MOBA_VROWS, MOBA_BLOCK), BF16),
        ],
        compiler_params=pltpu.CompilerParams(
            dimension_semantics=("arbitrary", "arbitrary"),
            vmem_limit_bytes=VMEM_LIMIT),
        name="moba_prep",
    )(moba, km)


def _moba_attn_kernel(qa_ref, ka_ref, vt_ref, o_ref, s_scr, p_scr, acc_scr):
    j = pl.program_id(2)
    qt = qa_ref.shape[3]
    ngrp = qt // MOBA_QG
    kb = MOBA_KT // MOBA_BLOCK
    dsteps = qt // MOBA_KT
    base = j * dsteps

    def keys(step):
        r0 = pl.multiple_of(step * MOBA_KT, MOBA_KT)
        return ka_ref[0, 0, pl.ds(r0, MOBA_KT), :]

    def score_stage(buf, step, g, mask_from=None):
        s = _dot(keys(step), qa_ref[0, 0, :, g * MOBA_QG:(g + 1) * MOBA_QG])
        if mask_from is not None:
            krow = lax.broadcasted_iota(jnp.int32, s.shape, 0) + mask_from
            qcol = lax.broadcasted_iota(jnp.int32, s.shape, 1) + g * MOBA_QG
            s = jnp.where(krow <= qcol, s, -jnp.inf)
        s_scr[buf] = s

    def softmax_stage(buf, pbuf, m):
        s = s_scr[buf]
        m_new = jnp.max(s, axis=0, keepdims=True)
        alpha = None
        if m is not None:
            m_new = jnp.maximum(m, m_new)
            alpha = jnp.exp(m - m_new)
        p_scr[pbuf] = jnp.exp(s - m_new).astype(BF16)
        return m_new, alpha

    def value_stage(buf, g, alpha, step):
        pv = _dot(vt_ref[0, 0, step * kb], p_scr[buf, 0:MOBA_BLOCK, :])
        for i in range(1, kb):
            pv = pv + _dot(vt_ref[0, 0, step * kb + i],
                           p_scr[buf, i * MOBA_BLOCK:(i + 1) * MOBA_BLOCK, :])
        acc_scr[g] = pv if alpha is None else alpha * acc_scr[g] + pv

    items = []
    for d in range(dsteps):
        k_lo, k_hi = d * MOBA_KT, (d + 1) * MOBA_KT - 1
        for g in range(ngrp):
            q_lo, q_hi = g * MOBA_QG, (g + 1) * MOBA_QG - 1
            if k_lo <= q_hi:
                items.append((d, g, k_lo if k_hi > q_lo else None))
    nbuf, ahead = s_scr.shape[0], MOBA_AHEAD
    assert ngrp % nbuf == 0 and ngrp % 2 == 0 and len(items) % 2 == 0 and ahead < nbuf
    assert items[-1][1] == ngrp - 1 and ahead <= ngrp
    par0 = len(items) % nbuf

    def diag_score(i):
        if i < len(items):
            score_stage(i % nbuf, base + items[i][0], items[i][1], items[i][2])
        else:
            score_stage(i % nbuf, 0, i - len(items))

    ms = [None] * ngrp
    for i in range(ahead):
        diag_score(i)
    pend = None
    for i, (d, g, _) in enumerate(items):
        diag_score(i + ahead)
        ms[g], alpha = softmax_stage(i % nbuf, i % 2, ms[g])
        if pend is not None:
            value_stage((i - 1) % 2, *pend)
        pend = (g, alpha, base + d)

    def body(n, carry):
        ms, alpha_p, step_p = carry
        ms = list(ms)
        pend = (ngrp - 1, alpha_p, step_p)
        for g in range(ngrp):
            if g + ahead < ngrp:
                score_stage((par0 + g + ahead) % nbuf, n, g + ahead)
            else:
                score_stage((par0 + g + ahead) % nbuf, jnp.minimum(n + 1, base - 1),
                            g + ahead - ngrp)
            ms[g], alpha = softmax_stage((par0 + g) % nbuf, g % 2, ms[g])
            value_stage((g - 1) % 2, *pend)
            pend = (g, alpha, n)
        return tuple(ms), pend[1], pend[2]

    _, alpha_p, step_p = lax.fori_loop(0, base, body, (tuple(ms), pend[1], pend[2]))
    value_stage(1, ngrp - 1, alpha_p, step_p)
    for g in range(ngrp):
        num, den = acc_scr[g, 0:MOBA_HD, :], acc_scr[g, MOBA_HD:MOBA_HD + 1, :]
        o_ref[0, :, g * MOBA_QG:(g + 1) * MOBA_QG] = (num / den).astype(o_ref.dtype)


def _moba_attn(qa, ka, vt):
    bsz, _, _, t_len = qa.shape
    nb = t_len // MOBA_BLOCK
    qt = min(MOBA_QT, t_len)
    return pl.pallas_call(
        _moba_attn_kernel,
        grid=(bsz, MOBA_HEADS, t_len // qt),
        in_specs=[
            pl.BlockSpec((1, 1, MOBA_AUG, qt), lambda b, h, t: (b, h, 0, t)),
            pl.BlockSpec((1, 1, t_len, MOBA_AUG), lambda b, h, t: (b, h, 0, 0)),
            pl.BlockSpec((1, 1, nb, MOBA_VROWS, MOBA_BLOCK), lambda b, h, t: (b, h, 0, 0, 0)),
        ],
        out_specs=pl.BlockSpec((1, MOBA_HD, qt), lambda b, h, t: (b, h, t)),
        out_shape=jax.ShapeDtypeStruct((bsz, MOBA_W, t_len), BF16),
        scratch_shapes=[
            pltpu.VMEM((MOBA_SBUF, MOBA_KT, MOBA_QG), F32),
            pltpu.VMEM((2, MOBA_KT, MOBA_QG), BF16),
            pltpu.VMEM((qt // MOBA_QG, MOBA_VROWS, MOBA_QG), F32),
        ],
        compiler_params=pltpu.CompilerParams(
            dimension_semantics=("arbitrary", "arbitrary", "arbitrary"),
            vmem_limit_bytes=VMEM_LIMIT),
        name="moba_attn",
    )(qa, ka, vt)


def _merge_kernel(x_ref, gpre_ref, bra_ref, brbt_ref, brc_ref, wgate_ref,
                  wa_ref, wb_ref, wc_ref, wout_ref, gpost_ref, o_ref):
    x = x_ref[0]
    h = _rms(x, gpre_ref[...]).astype(BF16)
    merged = jax.nn.sigmoid(_dot(h, wgate_ref[:, 0:D_MODEL])) * _dot(bra_ref[0], wa_ref[...])
    merged = merged + (jax.nn.sigmoid(_dot(h, wgate_ref[:, D_MODEL:2 * D_MODEL]))
                       * _dot_tn(brbt_ref[0], wb_ref[...]))
    merged = merged + (jax.nn.sigmoid(_dot(h, wgate_ref[:, 2 * D_MODEL:3 * D_MODEL]))
                       * _dot(brc_ref[0], wc_ref[...]))
    y = _dot(merged.astype(BF16), wout_ref[...])
    o_ref[0] = x + _rms(y, gpost_ref[...])


def _merge(x, gpre, bra, brbt, brc, wgate, wa, wb, wc, wout, gpost, tm):
    bsz, t_len, _ = x.shape
    const = lambda b, t: (0, 0)
    return pl.pallas_call(
        _merge_kernel,
        grid=(bsz, t_len // tm),
        in_specs=[
            pl.BlockSpec((1, tm, D_MODEL), lambda b, t: (b, t, 0)),
            pl.BlockSpec((1, D_MODEL), const),
            pl.BlockSpec((1, tm, GLA_DV), lambda b, t: (b, t, 0)),
            pl.BlockSpec((1, MOBA_W, tm), lambda b, t: (b, 0, t)),
            pl.BlockSpec((1, tm, SC_W), lambda b, t: (b, t, 0)),
            pl.BlockSpec(wgate.shape, const),
            pl.BlockSpec(wa.shape, const),
            pl.BlockSpec(wb.shape, const),
            pl.BlockSpec(wc.shape, const),
            pl.BlockSpec(wout.shape, const),
            pl.BlockSpec((1, D_MODEL), const),
        ],
        out_specs=pl.BlockSpec((1, tm, D_MODEL), lambda b, t: (b, t, 0)),
        out_shape=jax.ShapeDtypeStruct(x.shape, x.dtype),
        compiler_params=pltpu.CompilerParams(
            dimension_semantics=("arbitrary", "arbitrary"),
            vmem_limit_bytes=VMEM_LIMIT),
        name="merge",
    )(x, gpre, bra, brbt, brc, wgate, wa, wb, wc, wout, gpost)


def _ffn_kernel(x_ref, gpre_ref, wup_ref, cw_ref, cb_ref, wdn_ref, gpost_ref,
                o_ref, halo_ref):
    tm = x_ref.shape[1]

    @pl.when(pl.program_id(1) == 0)
    def _():
        halo_ref[...] = jnp.zeros_like(halo_ref)

    x = x_ref[0]
    h = _rms(x, gpre_ref[...]).astype(BF16)
    y = jnp.zeros((tm, D_MODEL), F32)
    for ci in range(D_FF // FFN_CHUNK):
        halves = []
        for base in (ci * FFN_CHUNK, D_FF + ci * FFN_CHUNK):
            cols = slice(base, base + FFN_CHUNK)
            u = _dot(h, wup_ref[:, cols])
            conv = _causal_conv3(u, halo_ref[:, cols], cw_ref[:, cols]) + cb_ref[:, cols]
            halo_ref[:, cols] = u[tm - 8:tm, :]
            halves.append(conv)
        act = (halves[0] * jax.nn.sigmoid(halves[0]) * halves[1]).astype(BF16)
        y = y + _dot(act, wdn_ref[ci * FFN_CHUNK:(ci + 1) * FFN_CHUNK, :])
    o_ref[0] = x + _rms(y, gpost_ref[...])


def _ffn(x, gpre, wup, cw, cb, wdn, gpost, tm):
    bsz, t_len, _ = x.shape
    const = lambda b, t: (0, 0)
    return pl.pallas_call(
        _ffn_kernel,
        grid=(bsz, t_len // tm),
        in_specs=[
            pl.BlockSpec((1, tm, D_MODEL), lambda b, t: (b, t, 0)),
            pl.BlockSpec((1, D_MODEL), const),
            pl.BlockSpec(wup.shape, const, pipeline_mode=pl.Buffered(1)),
            pl.BlockSpec(cw.shape, const),
            pl.BlockSpec(cb.shape, const),
            pl.BlockSpec(wdn.shape, const, pipeline_mode=pl.Buffered(1)),
            pl.BlockSpec((1, D_MODEL), const),
        ],
        out_specs=pl.BlockSpec((1, tm, D_MODEL), lambda b, t: (b, t, 0)),
        out_shape=jax.ShapeDtypeStruct(x.shape, x.dtype),
        scratch_shapes=[pltpu.VMEM((8, 2 * D_FF), F32)],
        compiler_params=pltpu.CompilerParams(
            dimension_semantics=("arbitrary", "arbitrary"),
            vmem_limit_bytes=VMEM_LIMIT),
        name="ffn",
    )(x, gpre, wup, cw, cb, wdn, gpost)


def _pad_rows(w, rows):
    return jnp.concatenate([w, jnp.zeros((rows - w.shape[0],) + w.shape[1:], w.dtype)], axis=0)


def kernel(x, g_mix_pre, w_in, gla_w_lr2, gla_b_lr, gla_norm, sc_conv_w, w_br_gla, w_br_moba,
           w_br_sc, w_out, g_mix_post, g_ffn_pre, ffn_w_up, ffn_conv_w, ffn_conv_b, ffn_w_down,
           g_ffn_post):
    bsz, t_len, _ = x.shape
    depth = w_in.shape[0]
    tm = min(512, t_len)
    tg = min(256, t_len)
    nb = t_len // MOBA_BLOCK
    gla_cols = 2 * GLA_DK + 2 * GLA_DV + GLA_LOWRANK
    moba_end = gla_cols + 3 * MOBA_W
    sc_end = moba_end + 3 * SC_W

    for l in range(depth):
        w = w_in[l]
        wg = jnp.concatenate(
            [w[:, :gla_cols], jnp.zeros((D_MODEL, GLA_LR_PAD - GLA_LOWRANK), w.dtype)],
            axis=1).astype(BF16)
        wm = w[:, gla_cols:moba_end].astype(BF16)
        ws = w[:, moba_end:sc_end].astype(BF16)
        wgate = w[:, sc_end:].astype(BF16)
        row = lambda a: a.reshape(1, -1)

        gla_in, moba, br_c, km = _inproj(x, row(g_mix_pre[l]), wg, wm, ws,
                                         _pad_rows(sc_conv_w[l], 8), tm)
        br_a = _gla(gla_in, _pad_rows(gla_w_lr2[l], GLA_LR_PAD), row(gla_b_lr[l]),
                    row(jnp.tile(gla_norm[l], GLA_HEADS)), tg)
        km = km.reshape(bsz, nb, MOBA_HEADS, MOBA_HD).transpose(0, 2, 1, 3)
        km = jnp.pad(km, ((0, 0), (0, 0), (0, MOBA_MAXB - nb), (0, 0)))
        qa, ka, vt = _moba_prep(moba, km)
        br_bt = _moba_attn(qa, ka, vt)
        x = _merge(x, row(g_mix_pre[l]), br_a, br_bt, br_c, wgate,
                   w_br_gla[l].astype(BF16), w_br_moba[l].astype(BF16),
                   w_br_sc[l].astype(BF16), w_out[l].astype(BF16), row(g_mix_post[l]), tm)
        x = _ffn(x, row(g_ffn_pre[l]), ffn_w_up[l].astype(BF16), _pad_rows(ffn_conv_w[l], 8),
                 row(ffn_conv_b[l]), ffn_w_down[l].astype(BF16), row(g_ffn_post[l]), tm)
    return x
```

```python
import functools

import jax
import jax.numpy as jnp
from jax import lax
from jax.experimental import pallas as pl
from jax.experimental.pallas import tpu as pltpu

F32 = jnp.float32
BF16 = jnp.bfloat16

D_MODEL = 1024
EPS = 1e-6

GLA_HEADS = 4
GLA_DK = 256
GLA_DV = 512
GLA_HK = GLA_DK // GLA_HEADS
GLA_HV = GLA_DV // GLA_HEADS
GLA_LOWRANK = 16
GLA_TAU = 16.0
GLA_CHUNK = 64
GLA_SUB = 8
GLA_LR_PAD = 128
GLA_IN = 2 * GLA_DK + 2 * GLA_DV + GLA_LR_PAD

MOBA_HEADS = 8
MOBA_HD = 64
MOBA_W = MOBA_HEADS * MOBA_HD
MOBA_BLOCK = 256
MOBA_TOPK = 3
MOBA_AUG = 2 * MOBA_HD
MOBA_MAXB = MOBA_AUG - MOBA_HD
MOBA_NEG = -65536.0
MOBA_QT = 4096
MOBA_KT = 512
MOBA_QG = 256
MOBA_SBUF = 4
MOBA_AHEAD = 3
MOBA_VROWS = MOBA_HD + 16

SC_W = 512
D_FF = 2816
FFN_CHUNK = 2816

VMEM_LIMIT = 56 * 1024 * 1024


def _rms(x, g):
    return x * lax.rsqrt(jnp.mean(x * x, axis=-1, keepdims=True) + EPS) * g


def _dot(a, b):
    return jnp.dot(a, b, preferred_element_type=F32)


def _dot_nt(a, b):
    return lax.dot_general(a, b, (((1,), (1,)), ((), ())), preferred_element_type=F32)


def _dot_tn(a, b):
    return lax.dot_general(a, b, (((0,), (0,)), ((), ())), preferred_element_type=F32)


def _split3(a):
    hi = a.astype(BF16)
    r1 = a - hi.astype(F32)
    mid = r1.astype(BF16)
    lo = (r1 - mid.astype(F32)).astype(BF16)
    return hi, mid, lo


def _causal_conv3(p, halo, w):
    row = lax.broadcasted_iota(jnp.int32, p.shape, 0)
    p1 = pltpu.roll(p, 1, 0)
    p1 = jnp.where(row == 0, halo[7:8, :], p1)
    p2 = pltpu.roll(p, 2, 0)
    p2 = jnp.where(row == 0, halo[6:7, :], jnp.where(row == 1, halo[7:8, :], p2))
    return w[0:1, :] * p2 + w[1:2, :] * p1 + w[2:3, :] * p


def _inproj_kernel(x_ref, g_ref, wg_ref, wm_ref, ws_ref, cw_ref,
                   gla_ref, moba_ref, brc_ref, km_ref, halo_ref):
    tm = x_ref.shape[1]

    @pl.when(pl.program_id(1) == 0)
    def _():
        halo_ref[...] = jnp.zeros_like(halo_ref)

    h = _rms(x_ref[0], g_ref[...]).astype(BF16)
    gla_ref[0] = _dot(h, wg_ref[...])
    mo = _dot(h, wm_ref[...])
    moba_ref[0] = mo
    kblk = mo[:, MOBA_W:2 * MOBA_W].reshape(tm // MOBA_BLOCK, MOBA_BLOCK, MOBA_W)
    km_ref[0, 0] = jnp.sum(kblk, axis=1) * (1.0 / MOBA_BLOCK)
    s = _dot(h, ws_ref[...])
    sb = s[:, 0:SC_W]
    p = s[:, SC_W:2 * SC_W] * s[:, 2 * SC_W:3 * SC_W]
    y = _causal_conv3(p, halo_ref[...], cw_ref[...])
    halo_ref[...] = p[tm - 8:tm, :]
    brc_ref[0] = (sb * y).astype(BF16)


def _inproj(x, g, wg, wm, ws, cw, tm):
    bsz, t_len, _ = x.shape
    const = lambda b, t: (0, 0)
    return pl.pallas_call(
        _inproj_kernel,
        grid=(bsz, t_len // tm),
        in_specs=[
            pl.BlockSpec((1, tm, D_MODEL), lambda b, t: (b, t, 0)),
            pl.BlockSpec((1, D_MODEL), const),
            pl.BlockSpec(wg.shape, const),
            pl.BlockSpec(wm.shape, const),
            pl.BlockSpec(ws.shape, const),
            pl.BlockSpec(cw.shape, const),
        ],
        out_specs=[
            pl.BlockSpec((1, tm, GLA_IN), lambda b, t: (b, t, 0)),
            pl.BlockSpec((1, tm, 3 * MOBA_W), lambda b, t: (b, t, 0)),
            pl.BlockSpec((1, tm, SC_W), lambda b, t: (b, t, 0)),
            pl.BlockSpec((1, 1, tm // MOBA_BLOCK, MOBA_W), lambda b, t: (b, t, 0, 0)),
        ],
        out_shape=[
            jax.ShapeDtypeStruct((bsz, t_len, GLA_IN), F32),
            jax.ShapeDtypeStruct((bsz, t_len, 3 * MOBA_W), F32),
            jax.ShapeDtypeStruct((bsz, t_len, SC_W), BF16),
            jax.ShapeDtypeStruct((bsz, t_len // tm, tm // MOBA_BLOCK, MOBA_W), F32),
        ],
        scratch_shapes=[pltpu.VMEM((8, SC_W), F32)],
        compiler_params=pltpu.CompilerParams(
            dimension_semantics=("arbitrary", "arbitrary"),
            vmem_limit_bytes=VMEM_LIMIT),
        name="inproj",
    )(x, g, wg, wm, ws, cw)


def _gla_kernel(z_ref, wlr_ref, blr_ref, gn_ref, o_ref, st_ref):
    tg = z_ref.shape[1]
    c = GLA_CHUNK

    @pl.when(pl.program_id(0) == 0)
    def _():
        st_ref[...] = jnp.zeros_like(st_ref)

    row = lax.broadcasted_iota(jnp.int32, (c, GLA_DK), 0)
    col_j = lax.broadcasted_iota(jnp.int32, (c, GLA_HEADS * c), 1) % c
    prow = lax.broadcasted_iota(jnp.int32, (c, GLA_HEADS * c), 0)
    arow = lax.broadcasted_iota(jnp.int32, (c, c), 0)
    acol = lax.broadcasted_iota(jnp.int32, (c, c), 1)
    tril = (arow >= acol).astype(BF16)
    ed = lax.broadcasted_iota(jnp.int32, (GLA_DK, GLA_HEADS * c), 0) // GLA_HK
    ej = lax.broadcasted_iota(jnp.int32, (GLA_DK, GLA_HEADS * c), 1) // c
    expand = (ed == ej).astype(BF16)
    kh = lax.broadcasted_iota(jnp.int32, (GLA_HEADS * c, GLA_DK), 0) // c
    kd_head = lax.broadcasted_iota(jnp.int32, (GLA_HEADS * c, GLA_DK), 1) // GLA_HK
    key_mask = kh == kd_head
    vh = lax.broadcasted_iota(jnp.int32, (GLA_HEADS * c, GLA_DV), 0) // c
    ve_head = lax.broadcasted_iota(jnp.int32, (GLA_HEADS * c, GLA_DV), 1) // GLA_HV
    val_mask = vh == ve_head
    se = lax.broadcasted_iota(jnp.int32, (GLA_DV, GLA_DK), 0) // GLA_HV
    sd = lax.broadcasted_iota(jnp.int32, (GLA_DV, GLA_DK), 1) // GLA_HK
    same_head = se == sd
    neg_inf = jnp.float32(-jnp.inf)
    w1, w2, w3 = _split3(wlr_ref[...])

    def one_batch(bi, rows):
        q = z_ref[bi, rows, 0:GLA_DK] * (GLA_HK ** -0.5)
        k = z_ref[bi, rows, GLA_DK:2 * GLA_DK]
        v = z_ref[bi, rows, 2 * GLA_DK:2 * GLA_DK + GLA_DV]
        r = z_ref[bi, rows, 2 * GLA_DK + GLA_DV:2 * GLA_DK + 2 * GLA_DV]
        lr = z_ref[bi, rows, 2 * GLA_DK + 2 * GLA_DV:GLA_IN]

        l1, l2, l3 = _split3(lr)
        zl = (_dot(l1, w1) + (_dot(l1, w2) + _dot(l2, w1))
              + (_dot(l1, w3) + _dot(l2, w2) + _dot(l3, w1))) + blr_ref[...]
        yield
        log_a = (jnp.minimum(zl, 0.0) - jnp.log(1.0 + jnp.exp(-jnp.abs(zl)))) * (1.0 / GLA_TAU)
        a1, a2, a3 = _split3(log_a)
        bc = _dot(tril, a1) + _dot(tril, a2) + _dot(tril, a3)
        yield

        att = jnp.zeros((c, GLA_HEADS * c), F32)
        s = c // 2
        while s >= GLA_SUB:
            ngrp = c // (2 * s)
            ref = jnp.concatenate(
                [jnp.broadcast_to(bc[g * 2 * s + s:g * 2 * s + s + 1, :], (2 * s, GLA_DK))
                 for g in range(ngrp)], axis=0)
            later = (row % (2 * s)) >= s
            ql = (q * jnp.exp(jnp.where(later, bc - ref, neg_inf))).astype(BF16)
            kl = (k * jnp.exp(jnp.where(later, neg_inf, ref - bc))).astype(BF16)
            kl_heads = jnp.where(key_mask, jnp.concatenate([kl] * GLA_HEADS, axis=0),
                                 jnp.zeros((), BF16))
            part = _dot_nt(ql, kl_heads)
            yield
            if ngrp > 1:
                part = jnp.where((prow // (2 * s)) == (col_j // (2 * s)), part, 0.0)
            att = att + part
            s //= 2

        prods = []
        for dlt in range(GLA_SUB):
            if dlt == 0:
                prods.append((q * k).astype(BF16))
            else:
                ok = (row % GLA_SUB) >= dlt
                kd = pltpu.roll(k, dlt, 0)
                bd = pltpu.roll(bc, dlt, 0)
                prods.append((q * kd * jnp.exp(jnp.where(ok, bc - bd, neg_inf))).astype(BF16))
        coef = _dot(jnp.concatenate(prods, axis=0), expand)
        yield
        for dlt in range(GLA_SUB):
            att = att + jnp.where(col_j == prow - dlt, coef[dlt * c:(dlt + 1) * c, :], 0.0)

        vb = v.astype(BF16)
        v_heads = jnp.where(val_mask, jnp.concatenate([vb] * GLA_HEADS, axis=0),
                            jnp.zeros((), BF16))
        o = _dot(att.astype(BF16), v_heads)
        yield
        st = st_ref[bi]
        o_state = _dot_nt((q * jnp.exp(bc)).astype(BF16), st.astype(BF16))
        yield
        o = o + o_state

        blast = bc[c - 1:c, :]
        kdec = (k * jnp.exp(blast - bc)).astype(BF16)
        upd = _dot_tn(vb, kdec)
        yield
        st_ref[bi] = st * jnp.exp(blast) + jnp.where(same_head, upd, 0.0)

        outs = []
        for hh in range(GLA_HEADS):
            oh = o[:, hh * GLA_HV:(hh + 1) * GLA_HV]
            outs.append(oh * lax.rsqrt(jnp.mean(oh * oh, axis=-1, keepdims=True) + EPS))
        on = jnp.concatenate(outs, axis=1) * gn_ref[...]
        o_ref[bi, rows, :] = (on * (r * jax.nn.sigmoid(r))).astype(o_ref.dtype)

    def chunk(ci, carry):
        rows = pl.ds(pl.multiple_of(ci * c, c), c)
        live = [one_batch(bi, rows) for bi in range(z_ref.shape[0])]
        while live:
            live = [gen for gen in live if next(gen, "done") != "done"]
        return carry

    lax.fori_loop(0, tg // c, chunk, 0)


def _gla(z, wlr, blr, gn, tg):
    bsz, t_len, _ = z.shape
    const = lambda t: (0, 0)
    return pl.pallas_call(
        _gla_kernel,
        grid=(t_len // tg,),
        in_specs=[
            pl.BlockSpec((bsz, tg, GLA_IN), lambda t: (0, t, 0)),
            pl.BlockSpec(wlr.shape, const),
            pl.BlockSpec(blr.shape, const),
            pl.BlockSpec(gn.shape, const),
        ],
        out_specs=pl.BlockSpec((bsz, tg, GLA_DV), lambda t: (0, t, 0)),
        out_shape=jax.ShapeDtypeStruct((bsz, t_len, GLA_DV), BF16),
        scratch_shapes=[pltpu.VMEM((bsz, GLA_DV, GLA_DK), F32)],
        compiler_params=pltpu.CompilerParams(
            dimension_semantics=("arbitrary",),
            vmem_limit_bytes=VMEM_LIMIT),
        name="gla",
    )(z, wlr, blr, gn)


def _kmean_kernel(k_ref, o_ref):
    nblk = k_ref.shape[1] // MOBA_BLOCK
    k = k_ref[0].reshape(nblk, MOBA_BLOCK, MOBA_W)
    o_ref[0] = jnp.sum(k, axis=1) * (1.0 / MOBA_BLOCK)


def _kmean(moba, rows):
    bsz, t_len, _ = moba.shape
    return pl.pallas_call(
        _kmean_kernel,
        grid=(bsz, t_len // rows),
        in_specs=[pl.BlockSpec((1, rows, MOBA_W), lambda b, t: (b, t, 1))],
        out_specs=pl.BlockSpec((1, rows // MOBA_BLOCK, MOBA_W), lambda b, t: (b, t, 0)),
        out_shape=jax.ShapeDtypeStruct((bsz, t_len // MOBA_BLOCK, MOBA_W), F32),
        compiler_params=pltpu.CompilerParams(
            dimension_semantics=("arbitrary", "arbitrary"),
            vmem_limit_bytes=VMEM_LIMIT),
        name="moba_kmean",
    )(moba)


def _moba_prep_kernel(m_ref, km_ref, qa_ref, ka_ref, vt_ref):
    nb = MOBA_MAXB
    n_own = pl.program_id(1)
    n_own_f = n_own.astype(F32)
    q = m_ref[0, :, 0:MOBA_W]
    k = m_ref[0, :, MOBA_W:2 * MOBA_W]
    v = m_ref[0, :, 2 * MOBA_W:3 * MOBA_W]
    qt = q.T * (MOBA_HD ** -0.5)
    vt_ref[0, :, 0, 0:MOBA_HD, :] = v.T.reshape(MOBA_HEADS, MOBA_HD, MOBA_BLOCK).astype(BF16)
    tail = lax.broadcasted_iota(jnp.int32, (MOBA_HEADS, MOBA_VROWS - MOBA_HD, MOBA_BLOCK), 1)
    vt_ref[0, :, 0, MOBA_HD:MOBA_VROWS, :] = (tail == 0).astype(BF16)

    brow = lax.broadcasted_iota(jnp.int32, (nb, MOBA_BLOCK), 0).astype(F32)
    neg_inf = jnp.float32(-jnp.inf)
    lane = lax.broadcasted_iota(jnp.int32, (MOBA_BLOCK, MOBA_AUG), 1)
    pos = lax.broadcasted_iota(jnp.int32, (MOBA_BLOCK, MOBA_AUG), 0).astype(F32)

    for hh in range(MOBA_HEADS):
        slope = 2.0 ** (-8.0 * (hh + 1) / MOBA_HEADS)
        qth = qt[hh * MOBA_HD:(hh + 1) * MOBA_HD, :]
        q1, q2, _ = _split3(qth)
        m1, m2, _ = _split3(km_ref[0, hh])
        gate = _dot(m1, q1) + (_dot(m1, q2) + _dot(m2, q1))
        gate = jnp.where(brow < n_own_f, gate, neg_inf)
        chosen = jnp.zeros((nb, MOBA_BLOCK), jnp.bool_)
        for _ in range(MOBA_TOPK):
            best = jnp.max(gate, axis=0, keepdims=True)
            first = jnp.min(jnp.where(gate == best, brow, float(nb)), axis=0, keepdims=True)
            hit = brow == first
            chosen = chosen | (hit & (best > neg_inf))
            gate = jnp.where(hit, neg_inf, gate)
        term = jnp.where(chosen, (slope * MOBA_BLOCK) * (brow - n_own_f), MOBA_NEG)
        term = jnp.where(brow == n_own_f, 0.0, term)
        term = jnp.where(brow == float(nb - 1), 1.0, term)
        halves = [qth, term] if hh % 2 == 0 else [term, qth]
        qa_ref[0, hh] = jnp.concatenate(halves, axis=0).astype(BF16)

        kpair = k[:, (hh // 2) * MOBA_AUG:(hh // 2 + 1) * MOBA_AUG]
        idx = lane - MOBA_HD if hh % 2 == 0 else lane
        extra = jnp.where(idx == nb - 1, slope * pos, (idx == n_own).astype(F32))
        keep = (lane < MOBA_HD) if hh % 2 == 0 else (lane >= MOBA_HD)
        ka_ref[0, hh] = jnp.where(keep, kpair, extra).astype(BF16)


def _moba_prep(moba, km):
    bsz, t_len, _ = moba.shape
    nb = t_len // MOBA_BLOCK
    return pl.pallas_call(
        _moba_prep_kernel,
        grid=(bsz, nb),
        in_specs=[
            pl.BlockSpec((1, MOBA_BLOCK, 3 * MOBA_W), lambda b, t: (b, t, 0)),
            pl.BlockSpec((1, MOBA_HEADS, MOBA_MAXB, MOBA_HD), lambda b, t: (b, 0, 0, 0)),
        ],
        out_specs=[
            pl.BlockSpec((1, MOBA_HEADS, MOBA_AUG, MOBA_BLOCK), lambda b, t: (b, 0, 0, t)),
            pl.BlockSpec((1, MOBA_HEADS, MOBA_BLOCK, MOBA_AUG), lambda b, t: (b, 0, t, 0)),
            pl.BlockSpec((1, MOBA_HEADS, 1, MOBA_VROWS, MOBA_BLOCK), lambda b, t: (b, 0, t, 0, 0)),
        ],
        out_shape=[
            jax.ShapeDtypeStruct((bsz, MOBA_HEADS, MOBA_AUG, t_len), BF16),
            jax.ShapeDtypeStruct((bsz, MOBA_HEADS, t_len, MOBA_AUG), BF16),
            jax.ShapeDtypeStruct((bsz, MOBA_HEADS, nb, MOBA_VROWS, MOBA_BLOCK), BF16),
        ],
        compiler_params=pltpu.CompilerParams(
            dimension_semantics=("arbitrary", "arbitrary"),
            vmem_limit_bytes=VMEM_LIMIT),
        name="moba_prep",
    )(moba, km)


def _moba_attn_kernel(qa_ref, ka_ref, vt_ref, o_ref, s_scr, p_scr, acc_scr):
    j = pl.program_id(2)
    qt = qa_ref.shape[3]
    ngrp = qt // MOBA_QG
    kb = MOBA_KT // MOBA_BLOCK
    dsteps = qt // MOBA_KT
    base = j * dsteps

    def keys(step):
        r0 = pl.multiple_of(step * MOBA_KT, MOBA_KT)
        return ka_ref[0, 0, pl.ds(r0, MOBA_KT), :]

    def score_stage(buf, step, g, mask_from=None):
        s = _dot(keys(step), qa_ref[0, 0, :, g * MOBA_QG:(g + 1) * MOBA_QG])
        if mask_from is not None:
            krow = lax.broadcasted_iota(jnp.int32, s.shape, 0) + mask_from
            qcol = lax.broadcasted_iota(jnp.int32, s.shape, 1) + g * MOBA_QG
            s = jnp.where(krow <= qcol, s, -jnp.inf)
        s_scr[buf] = s

    def softmax_stage(buf, pbuf, m):
        s = s_scr[buf]
        m_new = jnp.max(s, axis=0, keepdims=True)
        alpha = None
        if m is not None:
            m_new = jnp.maximum(m, m_new)
            alpha = jnp.exp(m - m_new)
        p_scr[pbuf] = jnp.exp(s - m_new).astype(BF16)
        return m_new, alpha

    def value_stage(buf, g, alpha, step):
        pv = _dot(vt_ref[0, 0, step * kb], p_scr[buf, 0:MOBA_BLOCK, :])
        for i in range(1, kb):
            pv = pv + _dot(vt_ref[0, 0, step * kb + i],
                           p_scr[buf, i * MOBA_BLOCK:(i + 1) * MOBA_BLOCK, :])
        acc_scr[g] = pv if alpha is None else alpha * acc_scr[g] + pv

    items = []
    for d in range(dsteps):
        k_lo, k_hi = d * MOBA_KT, (d + 1) * MOBA_KT - 1
        for g in range(ngrp):
            q_lo, q_hi = g * MOBA_QG, (g + 1) * MOBA_QG - 1
            if k_lo <= q_hi:
                items.append((d, g, k_lo if k_hi > q_lo else None))
    nbuf, ahead = s_scr.shape[0], MOBA_AHEAD
    assert ngrp % nbuf == 0 and ngrp % 2 == 0 and len(items) % 2 == 0 and ahead < nbuf
    assert items[-1][1] == ngrp - 1 and ahead <= ngrp
    par0 = len(items) % nbuf

    def diag_score(i):
        if i < len(items):
            score_stage(i % nbuf, base + items[i][0], items[i][1], items[i][2])
        else:
            score_stage(i % nbuf, 0, i - len(items))

    ms = [None] * ngrp
    for i in range(ahead):
        diag_score(i)
    pend = None
    for i, (d, g, _) in enumerate(items):
        diag_score(i + ahead)
        ms[g], alpha = softmax_stage(i % nbuf, i % 2, ms[g])
        if pend is not None:
            value_stage((i - 1) % 2, *pend)
        pend = (g, alpha, base + d)

    def body(n, carry):
        ms, alpha_p, step_p = carry
        ms = list(ms)
        pend = (ngrp - 1, alpha_p, step_p)
        for g in range(ngrp):
            if g + ahead < ngrp:
                score_stage((par0 + g + ahead) % nbuf, n, g + ahead)
            else:
                score_stage((par0 + g + ahead) % nbuf, jnp.minimum(n + 1, base - 1),
                            g + ahead - ngrp)
            ms[g], alpha = softmax_stage((par0 + g) % nbuf, g % 2, ms[g])
            value_stage((g - 1) % 2, *pend)
            pend = (g, alpha, n)
        return tuple(ms), pend[1], pend[2]

    _, alpha_p, step_p = lax.fori_loop(0, base, body, (tuple(ms), pend[1], pend[2]))
    value_stage(1, ngrp - 1, alpha_p, step_p)
    for g in range(ngrp):
        num, den = acc_scr[g, 0:MOBA_HD, :], acc_scr[g, MOBA_HD:MOBA_HD + 1, :]
        o_ref[0, :, g * MOBA_QG:(g + 1) * MOBA_QG] = (num / den).astype(o_ref.dtype)


def _moba_attn(qa, ka, vt):
    bsz, _, _, t_len = qa.shape
    nb = t_len // MOBA_BLOCK
    qt = min(MOBA_QT, t_len)
    return pl.pallas_call(
        _moba_attn_kernel,
        grid=(bsz, MOBA_HEADS, t_len // qt),
        in_specs=[
            pl.BlockSpec((1, 1, MOBA_AUG, qt), lambda b, h, t: (b, h, 0, t)),
            pl.BlockSpec((1, 1, t_len, MOBA_AUG), lambda b, h, t: (b, h, 0, 0)),
            pl.BlockSpec((1, 1, nb, MOBA_VROWS, MOBA_BLOCK), lambda b, h, t: (b, h, 0, 0, 0)),
        ],
        out_specs=pl.BlockSpec((1, MOBA_HD, qt), lambda b, h, t: (b, h, t)),
        out_shape=jax.ShapeDtypeStruct((bsz, MOBA_W, t_len), BF16),
        scratch_shapes=[
            pltpu.VMEM((MOBA_SBUF, MOBA_KT, MOBA_QG), F32),
            pltpu.VMEM((2, MOBA_KT, MOBA_QG), BF16),
            pltpu.VMEM((qt // MOBA_QG, MOBA_VROWS, MOBA_QG), F32),
        ],
        compiler_params=pltpu.CompilerParams(
            dimension_semantics=("arbitrary", "arbitrary", "arbitrary"),
            vmem_limit_bytes=VMEM_LIMIT),
        name="moba_attn",
    )(qa, ka, vt)


def _merge_kernel(x_ref, gpre_ref, bra_ref, brbt_ref, brc_ref, wgate_ref,
                  wa_ref, wb_ref, wc_ref, wout_ref, gpost_ref, o_ref):
    x = x_ref[0]
    h = _rms(x, gpre_ref[...]).astype(BF16)
    merged = jax.nn.sigmoid(_dot(h, wgate_ref[:, 0:D_MODEL])) * _dot(bra_ref[0], wa_ref[...])
    merged = merged + (jax.nn.sigmoid(_dot(h, wgate_ref[:, D_MODEL:2 * D_MODEL]))
                       * _dot_tn(brbt_ref[0], wb_ref[...]))
    merged = merged + (jax.nn.sigmoid(_dot(h, wgate_ref[:, 2 * D_MODEL:3 * D_MODEL]))
                       * _dot(brc_ref[0], wc_ref[...]))
    y = _dot(merged.astype(BF16), wout_ref[...])
    o_ref[0] = x + _rms(y, gpost_ref[...])


def _merge(x, gpre, bra, brbt, brc, wgate, wa, wb, wc, wout, gpost, tm):
    bsz, t_len, _ = x.shape
    const = lambda b, t: (0, 0)
    return pl.pallas_call(
        _merge_kernel,
        grid=(bsz, t_len // tm),
        in_specs=[
            pl.BlockSpec((1, tm, D_MODEL), lambda b, t: (b, t, 0)),
            pl.BlockSpec((1, D_MODEL), const),
            pl.BlockSpec((1, tm, GLA_DV), lambda b, t: (b, t, 0)),
            pl.BlockSpec((1, MOBA_W, tm), lambda b, t: (b, 0, t)),
            pl.BlockSpec((1, tm, SC_W), lambda b, t: (b, t, 0)),
            pl.BlockSpec(wgate.shape, const),
            pl.BlockSpec(wa.shape, const),
            pl.BlockSpec(wb.shape, const),
            pl.BlockSpec(wc.shape, const),
            pl.BlockSpec(wout.shape, const),
            pl.BlockSpec((1, D_MODEL), const),
        ],
        out_specs=pl.BlockSpec((1, tm, D_MODEL), lambda b, t: (b, t, 0)),
        out_shape=jax.ShapeDtypeStruct(x.shape, x.dtype),
        compiler_params=pltpu.CompilerParams(
            dimension_semantics=("arbitrary", "arbitrary"),
            vmem_limit_bytes=VMEM_LIMIT),
        name="merge",
    )(x, gpre, bra, brbt, brc, wgate, wa, wb, wc, wout, gpost)


def _ffn_kernel(x_ref, gpre_ref, wup_ref, cw_ref, cb_ref, wdn_ref, gpost_ref,
                o_ref, halo_ref):
    tm = x_ref.shape[1]

    @pl.when(pl.program_id(1) == 0)
    def _():
        halo_ref[...] = jnp.zeros_like(halo_ref)

    x = x_ref[0]
    h = _rms(x, gpre_ref[...]).astype(BF16)
    y = jnp.zeros((tm, D_MODEL), F32)
    for ci in range(D_FF // FFN_CHUNK):
        halves = []
        for base in (ci * FFN_CHUNK, D_FF + ci * FFN_CHUNK):
            cols = slice(base, base + FFN_CHUNK)
            u = _dot(h, wup_ref[:, cols])
            conv = _causal_conv3(u, halo_ref[:, cols], cw_ref[:, cols]) + cb_ref[:, cols]
            halo_ref[:, cols] = u[tm - 8:tm, :]
            halves.append(conv)
        act = (halves[0] * jax.nn.sigmoid(halves[0]) * halves[1]).astype(BF16)
        y = y + _dot(act, wdn_ref[ci * FFN_CHUNK:(ci + 1) * FFN_CHUNK, :])
    o_ref[0] = x + _rms(y, gpost_ref[...])


def _ffn(x, gpre, wup, cw, cb, wdn, gpost, tm):
    bsz, t_len, _ = x.shape
    const = lambda b, t: (0, 0)
    return pl.pallas_call(
        _ffn_kernel,
        grid=(bsz, t_len // tm),
        in_specs=[
            pl.BlockSpec((1, tm, D_MODEL), lambda b, t: (b, t, 0)),
            pl.BlockSpec((1, D_MODEL), const),
            pl.BlockSpec(wup.shape, const, pipeline_mode=pl.Buffered(1)),
            pl.BlockSpec(cw.shape, const),
            pl.BlockSpec(cb.shape, const),
            pl.BlockSpec(wdn.shape, const, pipeline_mode=pl.Buffered(1)),
            pl.BlockSpec((1, D_MODEL), const),
        ],
        out_specs=pl.BlockSpec((1, tm, D_MODEL), lambda b, t: (b, t, 0)),
        out_shape=jax.ShapeDtypeStruct(x.shape, x.dtype),
        scratch_shapes=[pltpu.VMEM((8, 2 * D_FF), F32)],
        compiler_params=pltpu.CompilerParams(
            dimension_semantics=("arbitrary", "arbitrary"),
            vmem_limit_bytes=VMEM_LIMIT),
        name="ffn",
    )(x, gpre, wup, cw, cb, wdn, gpost)


def _pad_rows(w, rows):
    return jnp.concatenate([w, jnp.zeros((rows - w.shape[0],) + w.shape[1:], w.dtype)], axis=0)


def kernel(x, g_mix_pre, w_in, gla_w_lr2, gla_b_lr, gla_norm, sc_conv_w, w_br_gla, w_br_moba,
           w_br_sc, w_out, g_mix_post, g_ffn_pre, ffn_w_up, ffn_conv_w, ffn_conv_b, ffn_w_down,
           g_ffn_post):
    bsz, t_len, _ = x.shape
    depth = w_in.shape[0]
    tm = min(512, t_len)
    tg = min(256, t_len)
    nb = t_len // MOBA_BLOCK
    gla_cols = 2 * GLA_DK + 2 * GLA_DV + GLA_LOWRANK
    moba_end = gla_cols + 3 * MOBA_W
    sc_end = moba_end + 3 * SC_W

    for l in range(depth):
        w = w_in[l]
        wg = jnp.concatenate(
            [w[:, :gla_cols], jnp.zeros((D_MODEL, GLA_LR_PAD - GLA_LOWRANK), w.dtype)],
            axis=1).astype(BF16)
        wm = w[:, gla_cols:moba_end].astype(BF16)
        ws = w[:, moba_end:sc_end].astype(BF16)
        wgate = w[:, sc_end:].astype(BF16)
        row = lambda a: a.reshape(1, -1)

        gla_in, moba, br_c, km = _inproj(x, row(g_mix_pre[l]), wg, wm, ws,
                                         _pad_rows(sc_conv_w[l], 8), tm)
        br_a = _gla(gla_in, _pad_rows(gla_w_lr2[l], GLA_LR_PAD), row(gla_b_lr[l]),
                    row(jnp.tile(gla_norm[l], GLA_HEADS)), tg)
        km = km.reshape(bsz, nb, MOBA_HEADS, MOBA_HD).transpose(0, 2, 1, 3)
        km = jnp.pad(km, ((0, 0), (0, 0), (0, MOBA_MAXB - nb), (0, 0)))
        qa, ka, vt = _moba_prep(moba, km)
        br_bt = _moba_attn(qa, ka, vt)
        x = _merge(x, row(g_mix_pre[l]), br_a, br_bt, br_c, wgate,
                   w_br_gla[l].astype(BF16), w_br_moba[l].astype(BF16),
                   w_br_sc[l].astype(BF16), w_out[l].astype(BF16), row(g_mix_post[l]), tm)
        x = _ffn(x, row(g_ffn_pre[l]), ffn_w_up[l].astype(BF16), _pad_rows(ffn_conv_w[l], 8),
                 row(ffn_conv_b[l]), ffn_w_down[l].astype(BF16), row(g_ffn_post[l]), tm)
    return x
```
